```python
import jax, jax.numpy as jnp
from jax import lax
import numpy as np

D_MODEL = 2048
BATCH = 8
SEQ = 4096
DEPTH = 4

N_MIXERS = 2
N_SSM_LAYERS = (DEPTH + 1) // 2
N_CONV_LAYERS = DEPTH // 2
SSM_GROUP = 16
SSM_GROUPS = D_MODEL // SSM_GROUP
SSM_STATE = 64
DT_MIN = 1e-3
DT_MAX = 1e-1
SSM_C_STD = 0.5
CONV_WIDTH = 3
N_EXPERT_GROUPS = 8
EXPERTS_PER_GROUP = 8
N_EXPERTS = N_EXPERT_GROUPS * EXPERTS_PER_GROUP
TOP_K_IN_GROUP = 2
D_EXPERT = 512
EXPERT_BLOCK = 128
RMS_EPS = 1e-6

kernel_name = 'hybrid_s5_shortconv_hmoe'


def rmsnorm(x, g):
    xf = x.astype(jnp.float32)
    xf = xf * lax.rsqrt(jnp.mean(xf * xf, axis=-1, keepdims=True) + RMS_EPS)
    return xf.astype(x.dtype) * g


def _ssm_combine(c1, c2):
    a1r, a1i, b1r, b1i = c1
    a2r, a2i, b2r, b2i = c2
    return (a2r * a1r - a2i * a1i,
            a2r * a1i + a2i * a1r,
            a2r * b1r - a2i * b1i + b2r,
            a2r * b1i + a2i * b1r + b2i)


def s5_mixer(u, a_re, a_im, log_dt, b_re, b_im, c_re, c_im, d_skip, w_glu_a, w_glu_b):
    f32 = jnp.float32
    a_re = a_re.astype(f32)
    a_im = a_im.astype(f32)
    dt = jnp.exp(log_dt.astype(f32))[:, None]
    mag = jnp.exp(a_re * dt)
    lb_re = mag * jnp.cos(a_im * dt)
    lb_im = mag * jnp.sin(a_im * dt)
    den = a_re * a_re + a_im * a_im
    nr = lb_re - 1.0
    ni = lb_im
    coef_re = ((nr * a_re + ni * a_im) / den)[..., None]
    coef_im = ((ni * a_re - nr * a_im) / den)[..., None]
    br = b_re.astype(f32)
    bi = b_im.astype(f32)
    bb_re = coef_re * br - coef_im * bi
    bb_im = coef_re * bi + coef_im * br
    cr = c_re.astype(f32)
    ci = c_im.astype(f32)
    bsz, seq, d = u.shape
    ug = u.astype(f32).reshape(bsz, seq, SSM_GROUPS, SSM_GROUP)

    def one_sequence(us):
        bu_re = jnp.einsum('lgh,gph->lgp', us, bb_re)
        bu_im = jnp.einsum('lgh,gph->lgp', us, bb_im)
        ar = jnp.broadcast_to(lb_re, bu_re.shape)
        ai = jnp.broadcast_to(lb_im, bu_im.shape)
        _, _, s_re, s_im = lax.associative_scan(_ssm_combine, (ar, ai, bu_re, bu_im), axis=0)
        return jnp.einsum('ghp,lgp->lgh', cr, s_re) - jnp.einsum('ghp,lgp->lgh', ci, s_im)

    y = lax.map(one_sequence, ug).reshape(bsz, seq, d)
    y = y + d_skip.astype(f32) * u.astype(f32)
    y = jax.nn.gelu(y.astype(u.dtype))
    return (y @ w_glu_a) * jax.nn.sigmoid(y @ w_glu_b)


def shortconv_mixer(u, w_in, conv_w, w_out):
    seq = u.shape[1]
    b_gate, c_gate, v = jnp.split(u @ w_in, 3, axis=-1)
    z = c_gate * v
    zp = jnp.pad(z, ((0, 0), (CONV_WIDTH - 1, 0), (0, 0)))
    y = conv_w[0] * zp[:, 0:seq]
    for k in range(1, CONV_WIDTH):
        y = y + conv_w[k] * zp[:, k:k + seq]
    return (b_gate * y) @ w_out


def hier_moe(h, w_group_router, w_expert_router, w_gate, w_up, w_down):
    bsz, seq, d = h.shape
    n_tok = bsz * seq
    xf = h.reshape(n_tok, d)
    g_logits = (xf @ w_group_router).astype(jnp.float32)
    g_prob = jax.nn.softmax(g_logits, axis=-1)
    g_idx = jnp.argmax(g_logits, axis=-1)
    g_w = jnp.take_along_axis(g_prob, g_idx[:, None], axis=1)
    e_logits = jnp.einsum('td,gde->tge', xf, w_expert_router).astype(jnp.float32)
    e_logits = jnp.take_along_axis(e_logits, g_idx[:, None, None], axis=1)[:, 0]
    top_l, top_i = lax.top_k(e_logits, TOP_K_IN_GROUP)
    gate = g_w * jax.nn.softmax(top_l, axis=-1)
    expert = g_idx[:, None] * EXPERTS_PER_GROUP + top_i

    n_asg = n_tok * TOP_K_IN_GROUP
    e_flat = expert.reshape(n_asg).astype(jnp.int32)
    order = jnp.argsort(e_flat)
    e_sorted = e_flat[order]
    counts = jnp.zeros((N_EXPERTS,), jnp.int32).at[e_flat].add(1)
    padded = (counts + EXPERT_BLOCK - 1) // EXPERT_BLOCK * EXPERT_BLOCK
    pad_end = jnp.cumsum(padded)
    pad_start = pad_end - padded
    start = jnp.cumsum(counts) - counts
    rank = jnp.arange(n_asg, dtype=jnp.int32) - start[e_sorted]
    dest = jnp.zeros((n_asg,), jnp.int32).at[order].set(pad_start[e_sorted] + rank)
    n_rows = n_asg + N_EXPERTS * EXPERT_BLOCK
    n_blocks = n_rows // EXPERT_BLOCK
    token = jnp.arange(n_asg, dtype=jnp.int32) // TOP_K_IN_GROUP
    xs = jnp.zeros((n_rows, d), h.dtype).at[dest].set(xf[token])
    block_start = jnp.arange(n_blocks, dtype=jnp.int32) * EXPERT_BLOCK
    block_expert = jnp.minimum(jnp.searchsorted(pad_end, block_start, side='right'), N_EXPERTS - 1)

    def run_block(args):
        xb, e = args
        return (jax.nn.silu(xb @ w_gate[e]) * (xb @ w_up[e])) @ w_down[e]

    ys = lax.map(run_block, (xs.reshape(n_blocks, EXPERT_BLOCK, d), block_expert)).reshape(n_rows, d)
    y = ys[dest].reshape(n_tok, TOP_K_IN_GROUP, d)
    out = jnp.einsum('tk,tkd->td', gate.astype(h.dtype), y)
    return out.reshape(bsz, seq, d)


def setup_inputs(seed: int = 0) -> dict:
    key = jax.random.key(seed)
    ks = jax.random.split(key, 24)
    f32 = jnp.float32
    D, G, P, H = D_MODEL, SSM_GROUPS, SSM_STATE, SSM_GROUP
    NS, NC = N_SSM_LAYERS, N_CONV_LAYERS
    n_idx = jnp.arange(P, dtype=f32)
    x = jax.random.normal(ks[0], (BATCH, SEQ, D), f32)
    norm_mix = 1.0 + 0.02 * jax.random.normal(ks[1], (DEPTH, D), f32)
    norm_ffn = 1.0 + 0.02 * jax.random.normal(ks[2], (DEPTH, D), f32)
    norm_final = 1.0 + 0.02 * jax.random.normal(ks[3], (D,), f32)
    ssm_a_re = -0.5 + 0.01 * jax.random.normal(ks[4], (NS, G, P), f32)
    ssm_a_im = jnp.pi * n_idx + 0.01 * jax.random.normal(ks[5], (NS, G, P), f32)
    ssm_log_dt = jax.random.uniform(ks[6], (NS, G), f32, minval=float(np.log(DT_MIN)), maxval=float(np.log(DT_MAX)))
    ssm_b_re = jax.random.normal(ks[7], (NS, G, P, H), f32) * H ** -0.5
    ssm_b_im = jax.random.normal(ks[8], (NS, G, P, H), f32) * H ** -0.5
    ssm_c_re = jax.random.normal(ks[9], (NS, G, H, P), f32) * SSM_C_STD
    ssm_c_im = jax.random.normal(ks[10], (NS, G, H, P), f32) * SSM_C_STD
    ssm_d = jax.random.normal(ks[11], (NS, D), f32)
    ssm_w_glu_a = jax.random.normal(ks[12], (NS, D, D), f32) * D ** -0.5
    ssm_w_glu_b = jax.random.normal(ks[13], (NS, D, D), f32) * D ** -0.5
    conv_w_in = jax.random.normal(ks[14], (NC, D, 3 * D), f32) * D ** -0.5
    conv_w = jax.random.normal(ks[15], (NC, CONV_WIDTH, D), f32) * CONV_WIDTH ** -0.5
    conv_w_out = jax.random.normal(ks[16], (NC, D, D), f32) * D ** -0.5
    moe_w_group_router = jax.random.normal(ks[17], (DEPTH, D, N_EXPERT_GROUPS), f32) * D ** -0.5
    moe_w_expert_router = jax.random.normal(ks[18], (DEPTH, N_EXPERT_GROUPS, D, EXPERTS_PER_GROUP), f32) * D ** -0.5
    moe_w_gate = jax.random.normal(ks[19], (DEPTH, N_EXPERTS, D, D_EXPERT), f32) * D ** -0.5
    moe_w_up = jax.random.normal(ks[20], (DEPTH, N_EXPERTS, D, D_EXPERT), f32) * D ** -0.5
    moe_w_down = jax.random.normal(ks[21], (DEPTH, N_EXPERTS, D_EXPERT, D), f32) * D_EXPERT ** -0.5
    return {'x': x, 'norm_mix': norm_mix, 'norm_ffn': norm_ffn, 'norm_final': norm_final,
            'ssm_a_re': ssm_a_re, 'ssm_a_im': ssm_a_im, 'ssm_log_dt': ssm_log_dt,
            'ssm_b_re': ssm_b_re, 'ssm_b_im': ssm_b_im, 'ssm_c_re': ssm_c_re, 'ssm_c_im': ssm_c_im,
            'ssm_d': ssm_d, 'ssm_w_glu_a': ssm_w_glu_a, 'ssm_w_glu_b': ssm_w_glu_b,
            'conv_w_in': conv_w_in, 'conv_w': conv_w, 'conv_w_out': conv_w_out,
            'moe_w_group_router': moe_w_group_router, 'moe_w_expert_router': moe_w_expert_router,
            'moe_w_gate': moe_w_gate, 'moe_w_up': moe_w_up, 'moe_w_down': moe_w_down}


def reference(x, norm_mix, norm_ffn, norm_final, ssm_a_re, ssm_a_im, ssm_log_dt,
              ssm_b_re, ssm_b_im, ssm_c_re, ssm_c_im, ssm_d, ssm_w_glu_a, ssm_w_glu_b,
              conv_w_in, conv_w, conv_w_out, moe_w_group_router, moe_w_expert_router,
              moe_w_gate, moe_w_up, moe_w_down):
    h = x
    for i in range(DEPTH):
        u = rmsnorm(h, norm_mix[i])
        j = i // N_MIXERS
        if i % N_MIXERS == 0:
            mix = s5_mixer(u, ssm_a_re[j], ssm_a_im[j], ssm_log_dt[j], ssm_b_re[j], ssm_b_im[j],
                           ssm_c_re[j], ssm_c_im[j], ssm_d[j], ssm_w_glu_a[j], ssm_w_glu_b[j])
        else:
            mix = shortconv_mixer(u, conv_w_in[j], conv_w[j], conv_w_out[j])
        h = h + mix
        h = h + hier_moe(rmsnorm(h, norm_ffn[i]), moe_w_group_router[i], moe_w_expert_router[i],
                         moe_w_gate[i], moe_w_up[i], moe_w_down[i])
    return rmsnorm(h, norm_final)
```

```python
import functools

import jax
import jax.numpy as jnp
from jax import lax
from jax.experimental import pallas as pl
from jax.experimental.pallas import tpu as pltpu

F32 = jnp.float32
BF16 = jnp.bfloat16
I32 = jnp.int32

LANES = 128
SUBLANES = 8
VMEM_LIMIT = 56 * 1024 * 1024
RMS_EPS = 1e-6
TOP_K = 2


def _cparams(sem):
    return pltpu.CompilerParams(dimension_semantics=sem, vmem_limit_bytes=VMEM_LIMIT)


def _rms(h, g):
    ms = jnp.mean(h * h, axis=-1, keepdims=True)
    return h * lax.rsqrt(ms + RMS_EPS) * g


def _norm_kernel(h_ref, g_ref, o_ref):
    o_ref[...] = _rms(h_ref[...], g_ref[...]).astype(o_ref.dtype)


def rmsnorm_rows(h, g, out_dtype, rows):
    n, d = h.shape
    return pl.pallas_call(
        _norm_kernel,
        out_shape=jax.ShapeDtypeStruct((n, d), out_dtype),
        grid=(n // rows,),
        in_specs=[pl.BlockSpec((rows, d), lambda i: (i, 0)),
                  pl.BlockSpec((1, d), lambda i: (0, 0))],
        out_specs=pl.BlockSpec((rows, d), lambda i: (i, 0)),
        compiler_params=_cparams(("parallel",)),
        name="rmsnorm",
    )(h, g.reshape(1, d))


def _zoh(a_re, a_im, log_dt):
    dt = jnp.exp(log_dt)
    mag = jnp.exp(a_re * dt)
    return mag * jnp.cos(a_im * dt), mag * jnp.sin(a_im * dt)


def _ssm_prep_kernel(are_ref, aim_ref, ldt_ref, bre_ref, bim_ref,
                     are2_ref, aim2_ref, ldt2_ref,
                     bbre_ref, bbim_ref, lre_ref, lim_ref):
    a_re = are_ref[...]
    a_im = aim_ref[...]
    lb_re, lb_im = _zoh(a_re, a_im, ldt_ref[...])
    den = a_re * a_re + a_im * a_im
    nr = lb_re - 1.0
    ni = lb_im
    coef_re = (nr * a_re + ni * a_im) / den
    coef_im = (ni * a_re - nr * a_im) / den
    br = bre_ref[...]
    bi = bim_ref[...]
    bbre_ref[...] = coef_re * br - coef_im * bi
    bbim_ref[...] = coef_re * bi + coef_im * br
    l_re, l_im = _zoh(are2_ref[...], aim2_ref[...], ldt2_ref[...])
    lre_ref[...] = l_re
    lim_ref[...] = l_im


def ssm_prep(a_re, a_im, log_dt, b_re, b_im):
    g, p, hh = b_re.shape
    rep = lambda a: jnp.repeat(a.astype(F32), hh, axis=1)
    ldt2 = jnp.broadcast_to(log_dt.astype(F32)[:, None], (g, p))
    outs = pl.pallas_call(
        _ssm_prep_kernel,
        out_shape=(jax.ShapeDtypeStruct((g, p * hh), F32), jax.ShapeDtypeStruct((g, p * hh), F32),
                   jax.ShapeDtypeStruct((g, p), F32), jax.ShapeDtypeStruct((g, p), F32)),
        name="ssm_prep",
    )(rep(a_re), rep(a_im), rep(ldt2), b_re.astype(F32).reshape(g, p * hh), b_im.astype(F32).reshape(g, p * hh),
      a_re.astype(F32), a_im.astype(F32), ldt2)
    bb_re, bb_im, l_re, l_im = outs
    return bb_re.reshape(g, p, hh), bb_im.reshape(g, p, hh), l_re, l_im


def _ssm_block_weights(bb_re, bb_im, c_re, c_im, l_re, l_im, gb):
    g, p, hh = bb_re.shape
    j = g // gb
    eye = jnp.eye(gb, dtype=F32)

    def emb_b(bb):
        return jnp.einsum('jgph,gk->jghkp', bb.reshape(j, gb, p, hh), eye).reshape(j, gb * hh, gb * p)

    def emb_c(c):
        return jnp.einsum('jghp,gk->jkpgh', c.reshape(j, gb, hh, p), eye).reshape(j, gb * p, gb * hh)

    wb = jnp.concatenate([emb_b(bb_re), emb_b(bb_im)], axis=2).astype(BF16)
    wc = jnp.concatenate([emb_c(c_re.astype(F32)), -emb_c(c_im.astype(F32))], axis=1).astype(BF16)
    lre = l_re.reshape(j, 1, gb * p)
    lim = l_im.reshape(j, 1, gb * p)
    return wb, wc, lre, lim


def _s5_kernel(u_ref, wb_ref, lre_ref, lim_ref, wc_ref, d_ref, o_ref,
               bu_ref, sre_ref, sim_ref, *, steps, nb, ns):
    @pl.when(pl.program_id(1) == 0)
    def _():
        sre_ref[...] = jnp.zeros_like(sre_ref)
        sim_ref[...] = jnp.zeros_like(sim_ref)

    u = u_ref[...]
    bu_ref[...] = jnp.dot(u.astype(BF16), wb_ref[...], preferred_element_type=F32)
    lr = jnp.broadcast_to(lre_ref[...], (nb, ns))
    li = jnp.broadcast_to(lim_ref[...], (nb, ns))

    def step(t, carry):
        sr, si = carry
        r0 = pl.multiple_of(t * nb, nb)
        br = bu_ref[pl.ds(r0, nb), 0:ns]
        bi = bu_ref[pl.ds(r0, nb), ns:2 * ns]
        nr = lr * sr - li * si + br
        ni = lr * si + li * sr + bi
        bu_ref[pl.ds(r0, nb), 0:ns] = nr
        bu_ref[pl.ds(r0, nb), ns:2 * ns] = ni
        return nr, ni

    sr, si = lax.fori_loop(0, steps, step, (sre_ref[...], sim_ref[...]))
    sre_ref[...] = sr
    sim_ref[...] = si
    y = jnp.dot(bu_ref[...].astype(BF16), wc_ref[...], preferred_element_type=F32)
    y = y + d_ref[...] * u
    o_ref[...] = jax.nn.gelu(y).astype(o_ref.dtype)


def s5_scan(u, wb, lre, lim, wc, d_skip, nb, steps):
    n, d = u.shape
    j, cb, ns2 = wb.shape
    ns = ns2 // 2
    rows = steps * nb
    return pl.pallas_call(
        functools.partial(_s5_kernel, steps=steps, nb=nb, ns=ns),
        out_shape=jax.ShapeDtypeStruct((n, d), BF16),
        grid=(j, n // rows),
        in_specs=[pl.BlockSpec((rows, cb), lambda jj, i: (i, jj)),
                  pl.BlockSpec((None, cb, ns2), lambda jj, i: (jj, 0, 0)),
                  pl.BlockSpec((None, 1, ns), lambda jj, i: (jj, 0, 0)),
                  pl.BlockSpec((None, 1, ns), lambda jj, i: (jj, 0, 0)),
                  pl.BlockSpec((None, ns2, cb), lambda jj, i: (jj, 0, 0)),
                  pl.BlockSpec((1, cb), lambda jj, i: (0, jj))],
        out_specs=pl.BlockSpec((rows, cb), lambda jj, i: (i, jj)),
        scratch_shapes=[pltpu.VMEM((rows, ns2), F32),
                        pltpu.VMEM((nb, ns), F32),
                        pltpu.VMEM((nb, ns), F32)],
        compiler_params=_cparams(("parallel", "arbitrary")),
        name="s5_scan",
    )(u, wb, lre, lim, wc, d_skip.reshape(1, d))


def _glu_kernel(y_ref, wa_ref, wb_ref, h_ref, o_ref):
    y = y_ref[...]
    a = jnp.dot(y, wa_ref[...], preferred_element_type=F32)
    b = jnp.dot(y, wb_ref[...], preferred_element_type=F32)
    o_ref[...] = h_ref[...] + a * jax.nn.sigmoid(b)


def glu_residual(y, wa, wb, h, rows, cols):
    n, d = y.shape
    dn = wa.shape[1]
    return pl.pallas_call(
        _glu_kernel,
        out_shape=jax.ShapeDtypeStruct((n, dn), F32),
        grid=(dn // cols, n // rows),
        in_specs=[pl.BlockSpec((rows, d), lambda c, i: (i, 0)),
                  pl.BlockSpec((d, cols), lambda c, i: (0, c)),
                  pl.BlockSpec((d, cols), lambda c, i: (0, c)),
                  pl.BlockSpec((rows, cols), lambda c, i: (i, c))],
        out_specs=pl.BlockSpec((rows, cols), lambda c, i: (i, c)),
        compiler_params=_cparams(("parallel", "parallel")),
        name="glu_residual",
    )(y, wa, wb, h)


def _mm_res_kernel(x_ref, w_ref, h_ref, o_ref):
    o_ref[...] = h_ref[...] + jnp.dot(x_ref[...], w_ref[...], preferred_element_type=F32)


def matmul_residual(x, w, h, rows, cols):
    n, d = x.shape
    dn = w.shape[1]
    return pl.pallas_call(
        _mm_res_kernel,
        out_shape=jax.ShapeDtypeStruct((n, dn), F32),
        grid=(dn // cols, n // rows),
        in_specs=[pl.BlockSpec((rows, d), lambda c, i: (i, 0)),
                  pl.BlockSpec((d, cols), lambda c, i: (0, c)),
                  pl.BlockSpec((rows, cols), lambda c, i: (i, c))],
        out_specs=pl.BlockSpec((rows, cols), lambda c, i: (i, c)),
        compiler_params=_cparams(("parallel", "parallel")),
        name="matmul_residual",
    )(x, w, h)


def _conv_in_kernel(u_ref, wbg_ref, wcg_ref, wv_ref, cw_ref, o_ref, z_ref, *, rows, nb, width):
    halo = (width - 1) * nb

    @pl.when(pl.program_id(1) == 0)
    def _():
        z_ref[0:halo, :] = jnp.zeros((halo, z_ref.shape[1]), F32)

    u = u_ref[...]
    bg = jnp.dot(u, wbg_ref[...], preferred_element_type=F32)
    cg = jnp.dot(u, wcg_ref[...], preferred_element_type=F32)
    v = jnp.dot(u, wv_ref[...], preferred_element_type=F32)
    z_ref[halo:halo + rows, :] = cg * v
    y = cw_ref[0:1, :] * z_ref[0:rows, :]
    for k in range(1, width):
        y = y + cw_ref[k:k + 1, :] * z_ref[k * nb:k * nb + rows, :]
    o_ref[...] = (bg * y).astype(o_ref.dtype)
    z_ref[0:halo, :] = z_ref[rows:rows + halo, :]


def conv_in(u, w_in, conv_w, nb, rows, cols):
    n, d = u.shape
    width = conv_w.shape[0]
    nc = d // cols
    halo = (width - 1) * nb
    return pl.pallas_call(
        functools.partial(_conv_in_kernel, rows=rows, nb=nb, width=width),
        out_shape=jax.ShapeDtypeStruct((n, d), BF16),
        grid=(nc, n // rows),
        in_specs=[pl.BlockSpec((rows, d), lambda c, i: (i, 0)),
                  pl.BlockSpec((d, cols), lambda c, i: (0, c)),
                  pl.BlockSpec((d, cols), lambda c, i: (0, c + nc)),
                  pl.BlockSpec((d, cols), lambda c, i: (0, c + 2 * nc)),
                  pl.BlockSpec((width, cols), lambda c, i: (0, c))],
        out_specs=pl.BlockSpec((rows, cols), lambda c, i: (i, c)),
        scratch_shapes=[pltpu.VMEM((rows + halo, cols), F32)],
        compiler_params=_cparams(("parallel", "arbitrary")),
        name="conv_in",
    )(u, w_in, w_in, w_in, conv_w)


def _router_kernel(h_ref, g_ref, whi_ref, wlo_ref, tri_ref, xn_ref, mf_ref, mi_ref, cnt_ref,
                   *, rows, d, n_groups, epg):
    @pl.when(pl.program_id(0) == 0)
    def _():
        cnt_ref[...] = jnp.zeros_like(cnt_ref)

    xn = _rms(h_ref[...], g_ref[...])
    chunks = d // LANES
    for c in range(chunks):
        xn_ref[pl.ds(c, rows, stride=chunks), :] = xn[:, c * LANES:(c + 1) * LANES]

    x_hi = xn.astype(BF16)
    x_lo = (xn - x_hi.astype(F32)).astype(BF16)
    w_hi = whi_ref[...]
    logits = (jnp.dot(x_hi, w_hi, preferred_element_type=F32)
              + jnp.dot(x_lo, w_hi, preferred_element_type=F32)
              + jnp.dot(x_hi, wlo_ref[...], preferred_element_type=F32))

    lane = lax.broadcasted_iota(I32, (rows, LANES), 1).astype(F32)
    neg = jnp.float32(-jnp.inf)
    big = jnp.float32(LANES)

    def first_max(vals, mask):
        m = jnp.max(jnp.where(mask, vals, neg), axis=-1, keepdims=True)
        idx = jnp.min(jnp.where(mask & (vals == m), lane, big), axis=-1, keepdims=True)
        return m, idx

    gmask = lane < n_groups
    gmax, gidx = first_max(logits, gmask)
    gsum = jnp.sum(jnp.where(gmask, jnp.exp(logits - gmax), 0.0), axis=-1, keepdims=True)
    g_w = 1.0 / gsum
    lo = n_groups + gidx * epg
    emask = (lane >= lo) & (lane < lo + epg)
    t1, i1 = first_max(logits, emask)
    t2, i2 = first_max(logits, emask & (lane != i1))
    e2 = jnp.exp(t2 - t1)
    p1 = 1.0 / (1.0 + e2)
    p2 = e2 / (1.0 + e2)
    eid1 = i1 - n_groups
    eid2 = i2 - n_groups

    onehot = ((lane == eid1) | (lane == eid2))
    oh = jnp.where(onehot, 1.0, 0.0).astype(BF16)
    before = jnp.dot(tri_ref[...], oh, preferred_element_type=F32) + cnt_ref[...]
    r1 = jnp.sum(jnp.where(lane == eid1, before, 0.0), axis=-1, keepdims=True)
    r2 = jnp.sum(jnp.where(lane == eid2, before, 0.0), axis=-1, keepdims=True)
    cnt_ref[...] = cnt_ref[...] + jnp.sum(oh.astype(F32), axis=0, keepdims=True)

    mf_ref[...] = jnp.where(lane == 0, g_w * p1, jnp.where(lane == 1, g_w * p2, 0.0))
    mi_ref[...] = jnp.where(lane == 0, eid1, jnp.where(lane == 1, eid2,
                            jnp.where(lane == 2, r1, jnp.where(lane == 3, r2, 0.0)))).astype(I32)


def router(h, g, w_group, w_expert, rows):
    n, d = h.shape
    n_groups = w_group.shape[1]
    epg = w_expert.shape[2]
    n_exp = n_groups * epg
    assert n_groups + n_exp <= LANES
    chunks = d // LANES
    w = jnp.concatenate([w_group.astype(F32),
                         jnp.transpose(w_expert.astype(F32), (1, 0, 2)).reshape(d, n_exp)], axis=1)
    w = jnp.pad(w, ((0, 0), (0, LANES - n_groups - n_exp)))
    w_hi = w.astype(BF16)
    w_lo = (w - w_hi.astype(F32)).astype(BF16)
    tri = (lax.broadcasted_iota(I32, (rows, rows), 1) < lax.broadcasted_iota(I32, (rows, rows), 0)).astype(BF16)
    xn, mf, mi, cnt = pl.pallas_call(
        functools.partial(_router_kernel, rows=rows, d=d, n_groups=n_groups, epg=epg),
        out_shape=(jax.ShapeDtypeStruct((n * chunks, LANES), F32),
                   jax.ShapeDtypeStruct((n, LANES), F32),
                   jax.ShapeDtypeStruct((n, LANES), I32),
                   jax.ShapeDtypeStruct((1, LANES), F32)),
        grid=(n // rows,),
        in_specs=[pl.BlockSpec((rows, d), lambda i: (i, 0)),
                  pl.BlockSpec((1, d), lambda i: (0, 0)),
                  pl.BlockSpec((d, LANES), lambda i: (0, 0)),
                  pl.BlockSpec((d, LANES), lambda i: (0, 0)),
                  pl.BlockSpec((rows, rows), lambda i: (0, 0))],
        out_specs=(pl.BlockSpec((rows * chunks, LANES), lambda i: (i, 0)),
                   pl.BlockSpec((rows, LANES), lambda i: (i, 0)),
                   pl.BlockSpec((rows, LANES), lambda i: (i, 0)),
                   pl.BlockSpec((1, LANES), lambda i: (0, 0))),
        compiler_params=_cparams(("arbitrary",)),
        name="moe_router",
    )(h, g.reshape(1, d), w_hi, w_lo, tri)
    return xn, mf, mi, cnt


def _dispatch_kernel(pstart_ref, er_ref, xn_ref, xs_in_ref, xs_ref, sem, *, toks, chunks):
    del xs_in_ref
    i = pl.program_id(0)

    def row_copy(r, slot):
        e = er_ref[0, 0, 4 * r + slot]
        dst = pstart_ref[e] + er_ref[0, 0, 4 * r + 2 + slot]
        src = i * toks + r
        return pltpu.make_async_copy(xn_ref.at[pl.ds(pl.multiple_of(src * chunks, chunks), chunks)],
                                     xs_ref.at[pl.ds(pl.multiple_of(dst * chunks, chunks), chunks)], sem)

    def issue(r, c):
        row_copy(r, 0).start()
        row_copy(r, 1).start()
        return c

    lax.fori_loop(0, toks, issue, 0)

    def drain(r, c):
        row_copy(r, 0).wait()
        row_copy(r, 1).wait()
        return c

    lax.fori_loop(0, toks, drain, 0)


def moe_dispatch(pad_start, er, xn_lin, n_rows, toks, chunks):
    n = xn_lin.shape[0] // chunks
    xs0 = jnp.zeros((n_rows * chunks, LANES), F32)
    return pl.pallas_call(
        functools.partial(_dispatch_kernel, toks=toks, chunks=chunks),
        out_shape=jax.ShapeDtypeStruct((n_rows * chunks, LANES), F32),
        grid_spec=pltpu.PrefetchScalarGridSpec(
            num_scalar_prefetch=1,
            grid=(n // toks,),
            in_specs=[pl.BlockSpec((1, 1, 4 * toks), lambda i, ps: (i, 0, 0), memory_space=pltpu.SMEM),
                      pl.BlockSpec(memory_space=pl.ANY),
                      pl.BlockSpec(memory_space=pl.ANY)],
            out_specs=pl.BlockSpec(memory_space=pl.ANY),
            scratch_shapes=[pltpu.SemaphoreType.DMA(())]),
        input_output_aliases={3: 0},
        compiler_params=_cparams(("arbitrary",)),
        name="moe_dispatch",
    )(pad_start, er, xn_lin, xs0)


def _moe_ffn_kernel(be_ref, nu_ref, xs_ref, wg_ref, wu_ref, wd_ref, ys_ref,
                    wgb_ref, wub_ref, wdb_ref, *, bm, chunks):
    i = pl.program_id(0)
    prev = be_ref[jnp.maximum(i - 1, 0)]

    @pl.when((i == 0) | (be_ref[i] != prev))
    def _():
        wgb_ref[...] = wg_ref[...].astype(BF16)
        wub_ref[...] = wu_ref[...].astype(BF16)
        wdb_ref[...] = wd_ref[...].astype(BF16)

    @pl.when(i < nu_ref[0])
    def _():
        x = jnp.concatenate([xs_ref[pl.ds(c, bm, stride=chunks), :] for c in range(chunks)], axis=-1).astype(BF16)
        gt = jnp.dot(x, wgb_ref[...], preferred_element_type=F32)
        up = jnp.dot(x, wub_ref[...], preferred_element_type=F32)
        mid = (jax.nn.silu(gt) * up).astype(BF16)
        y = jnp.dot(mid, wdb_ref[...], preferred_element_type=F32)
        for c in range(chunks):
            ys_ref[pl.ds(c, bm, stride=chunks), :] = y[:, c * LANES:(c + 1) * LANES]

    @pl.when(i >= nu_ref[0])
    def _():
        ys_ref[...] = jnp.zeros_like(ys_ref)


def moe_ffn(blk_expert, n_used, xs_lin, w_gate, w_up, w_down, layer, bm, chunks):
    n_blocks = xs_lin.shape[0] // (bm * chunks)
    _, _, d, de = w_gate.shape

    def row_map(i, be, nu):
        return (jnp.minimum(i, nu[0] - 1), 0)

    return pl.pallas_call(
        functools.partial(_moe_ffn_kernel, bm=bm, chunks=chunks),
        out_shape=jax.ShapeDtypeStruct(xs_lin.shape, F32),
        grid_spec=pltpu.PrefetchScalarGridSpec(
            num_scalar_prefetch=2,
            grid=(n_blocks,),
            in_specs=[pl.BlockSpec((bm * chunks, LANES), row_map),
                      pl.BlockSpec((None, None, d, de), lambda i, be, nu: (layer, be[i], 0, 0)),
                      pl.BlockSpec((None, None, d, de), lambda i, be, nu: (layer, be[i], 0, 0)),
                      pl.BlockSpec((None, None, de, d), lambda i, be, nu: (layer, be[i], 0, 0))],
            out_specs=pl.BlockSpec((bm * chunks, LANES), lambda i, be, nu: (i, 0)),
            scratch_shapes=[pltpu.VMEM((d, de), BF16), pltpu.VMEM((d, de), BF16), pltpu.VMEM((de, d), BF16)]),
        compiler_params=_cparams(("arbitrary",)),
        name="moe_ffn",
    )(blk_expert, n_used, xs_lin, w_gate, w_up, w_down)


def _combine_kernel(pstart_ref, erc_ref, ern_ref, ys_ref, h_ref, mf_ref, g_ref, ho_ref, uo_ref,
                    buf_ref, sem, *, toks, chunks, n_tiles):
    i = pl.program_id(0)
    slot_rows = toks * TOP_K * chunks
    stride = TOP_K * chunks

    def buf_rows(slot, r, k):
        dst = slot * slot_rows + (r * TOP_K + k) * chunks
        return buf_ref.at[pl.ds(pl.multiple_of(dst, chunks), chunks)]

    def start_tile(er_ref, slot):
        def body(r, c):
            for k in range(TOP_K):
                src = pstart_ref[er_ref[0, 0, 4 * r + k]] + er_ref[0, 0, 4 * r + 2 + k]
                pltpu.make_async_copy(ys_ref.at[pl.ds(pl.multiple_of(src * chunks, chunks), chunks)],
                                      buf_rows(slot, r, k), sem.at[slot]).start()
            return c

        lax.fori_loop(0, toks, body, 0)

    def wait_tile(slot):
        def body(r, c):
            for k in range(TOP_K):
                pltpu.make_async_copy(ys_ref.at[pl.ds(0, chunks)], buf_rows(slot, r, k), sem.at[slot]).wait()
            return c

        lax.fori_loop(0, toks, body, 0)

    def run(slot):
        if slot == 0:
            @pl.when(i == 0)
            def _():
                start_tile(erc_ref, 0)

        @pl.when(i + 1 < n_tiles)
        def _():
            start_tile(ern_ref, 1 - slot)

        wait_tile(slot)
        base = slot * slot_rows
        y0 = jnp.concatenate([buf_ref[pl.ds(base + c, toks, stride=stride), :] for c in range(chunks)], axis=-1)
        y1 = jnp.concatenate([buf_ref[pl.ds(base + chunks + c, toks, stride=stride), :] for c in range(chunks)],
                             axis=-1)
        mf = mf_ref[...]
        h = h_ref[...] + mf[:, 0:1] * y0 + mf[:, 1:2] * y1
        ho_ref[...] = h
        uo_ref[...] = _rms(h, g_ref[...]).astype(uo_ref.dtype)

    @pl.when(i % 2 == 0)
    def _():
        run(0)

    @pl.when(i % 2 == 1)
    def _():
        run(1)


def moe_combine(pad_start, er_all, ys_lin, h, mf, g_next, u_dtype, toks, chunks):
    n, d = h.shape
    n_tiles = n // toks
    return pl.pallas_call(
        functools.partial(_combine_kernel, toks=toks, chunks=chunks, n_tiles=n_tiles),
        out_shape=(jax.ShapeDtypeStruct((n, d), F32), jax.ShapeDtypeStruct((n, d), u_dtype)),
        grid_spec=pltpu.PrefetchScalarGridSpec(
            num_scalar_prefetch=1,
            grid=(n_tiles,),
            in_specs=[pl.BlockSpec((1, 1, 4 * toks), lambda i, ps: (i, 0, 0), memory_space=pltpu.SMEM),
                      pl.BlockSpec((1, 1, 4 * toks), lambda i, ps: (jnp.minimum(i + 1, n_tiles - 1), 0, 0),
                                   memory_space=pltpu.SMEM),
                      pl.BlockSpec(memory_space=pl.ANY),
                      pl.BlockSpec((toks, d), lambda i, ps: (i, 0)),
                      pl.BlockSpec((toks, LANES), lambda i, ps: (i, 0)),
                      pl.BlockSpec((1, d), lambda i, ps: (0, 0))],
            out_specs=(pl.BlockSpec((toks, d), lambda i, ps: (i, 0)),
                       pl.BlockSpec((toks, d), lambda i, ps: (i, 0))),
            scratch_shapes=[pltpu.VMEM((2 * toks * TOP_K * chunks, LANES), F32),
                            pltpu.SemaphoreType.DMA((2,))]),
        compiler_params=_cparams(("arbitrary",)),
        name="moe_combine",
    )(pad_start, er_all, er_all, ys_lin, h, mf, g_next.reshape(1, d))


def hier_moe_layer(h, g_ffn, w_group, w_expert, w_gate, w_up, w_down, layer, g_next, u_dtype,
                   *, rows, bm, toks_d, toks_c):
    n, d = h.shape
    chunks = d // LANES
    n_exp = w_gate.shape[1]
    xn_lin, mf, mi, cnt = router(h, g_ffn, w_group, w_expert, rows)

    counts = cnt[0, :n_exp].astype(I32)
    padded = (counts + bm - 1) // bm * bm
    pad_end = jnp.cumsum(padded)
    pad_start = pad_end - padded
    n_rows = n * TOP_K + n_exp * bm
    n_blocks = n_rows // bm
    blk_start = jnp.arange(n_blocks, dtype=I32) * bm
    blk_expert = jnp.sum((blk_start[:, None] >= pad_end[None, :]).astype(I32), axis=1)
    blk_expert = jnp.minimum(blk_expert, n_exp - 1).astype(I32)
    n_used = (pad_end[-1:] // bm).astype(I32)
    last_e = blk_expert[jnp.maximum(n_used[0] - 1, 0)]
    blk_expert = jnp.where(jnp.arange(n_blocks) < n_used[0], blk_expert, last_e)

    er = mi[:, :4]
    xs_lin = moe_dispatch(pad_start, er.reshape(n // toks_d, 1, 4 * toks_d), xn_lin, n_rows, toks_d, chunks)
    ys_lin = moe_ffn(blk_expert, n_used, xs_lin, w_gate, w_up, w_down, layer, bm, chunks)
    return moe_combine(pad_start, er.reshape(n // toks_c, 1, 4 * toks_c), ys_lin, h, mf, g_next, u_dtype,
                       toks_c, chunks)


def trunk(x, norm_mix, norm_ffn, norm_final, ssm_a_re, ssm_a_im, ssm_log_dt, ssm_b_re, ssm_b_im,
          ssm_c_re, ssm_c_im, ssm_d, ssm_w_glu_a, ssm_w_glu_b, conv_w_in, conv_w, conv_w_out,
          moe_w_group_router, moe_w_expert_router, moe_w_gate, moe_w_up, moe_w_down,
          *, rows, steps, cols, bm, toks_d, toks_c, cblock):
    bsz, seq, d = x.shape
    depth = norm_mix.shape[0]
    n = bsz * seq
    hh = ssm_b_re.shape[3]
    gb = cblock // hh

    h = jnp.transpose(x, (1, 0, 2)).reshape(n, d)
    u = rmsnorm_rows(h, norm_mix[0], F32, rows)
    for i in range(depth):
        j = i // 2
        if i % 2 == 0:
            bb_re, bb_im, l_re, l_im = ssm_prep(ssm_a_re[j], ssm_a_im[j], ssm_log_dt[j], ssm_b_re[j], ssm_b_im[j])
            wb, wc, lre, lim = _ssm_block_weights(bb_re, bb_im, ssm_c_re[j], ssm_c_im[j], l_re, l_im, gb)
            yg = s5_scan(u, wb, lre, lim, wc, ssm_d[j].astype(F32), bsz, steps)
            h = glu_residual(yg, ssm_w_glu_a[j].astype(BF16), ssm_w_glu_b[j].astype(BF16), h, rows, cols)
        else:
            gy = conv_in(u, conv_w_in[j].astype(BF16), conv_w[j].astype(F32), bsz, rows, cols)
            h = matmul_residual(gy, conv_w_out[j].astype(BF16), h, rows, cols)
        last = i == depth - 1
        g_next = norm_final if last else norm_mix[i + 1]
        u_dtype = F32 if (last or (i + 1) % 2 == 0) else BF16
        h, u = hier_moe_layer(h, norm_ffn[i], moe_w_group_router[i], moe_w_expert_router[i],
                              moe_w_gate, moe_w_up, moe_w_down, i, g_next, u_dtype,
                              rows=rows, bm=bm, toks_d=toks_d, toks_c=toks_c)
    return jnp.transpose(u.reshape(seq, bsz, d), (1, 0, 2))


def kernel(x, norm_mix, norm_ffn, norm_final, ssm_a_re, ssm_a_im, ssm_log_dt, ssm_b_re, ssm_b_im, ssm_c_re, ssm_c_im, ssm_d, ssm_w_glu_a, ssm_w_glu_b, conv_w_in, conv_w, conv_w_out, moe_w_group_router, moe_w_expert_router, moe_w_gate, moe_w_up, moe_w_down):
    return trunk(x, norm_mix, norm_ffn, norm_final, ssm_a_re, ssm_a_im, ssm_log_dt, ssm_b_re, ssm_b_im,
                 ssm_c_re, ssm_c_im, ssm_d, ssm_w_glu_a, ssm_w_glu_b, conv_w_in, conv_w, conv_w_out,
                 moe_w_group_router, moe_w_expert_router, moe_w_gate, moe_w_up, moe_w_down,
                 rows=512, steps=64, cols=512, bm=256, toks_d=256, toks_c=128, cblock=256)
```

```python
import functools

import jax
import jax.numpy as jnp
from jax import lax
from jax.experimental import pallas as pl
from jax.experimental.pallas import tpu as pltpu

F32 = jnp.float32
BF16 = jnp.bfloat16
I32 = jnp.int32

LANES = 128
SUBLANES = 8
VMEM_LIMIT = 56 * 1024 * 1024
RMS_EPS = 1e-6
TOP_K = 2


def _cparams(sem):
    return pltpu.CompilerParams(dimension_semantics=sem, vmem_limit_bytes=VMEM_LIMIT)


def _rms(h, g):
    ms = jnp.mean(h * h, axis=-1, keepdims=True)
    return h * lax.rsqrt(ms + RMS_EPS) * g


def _norm_in_kernel(x_ref, g_ref, h_ref, u_ref, t_ref, *, nb, steps):
    chunks = t_ref.shape[0]
    for b in range(nb):
        xb = x_ref[b]
        for c in range(chunks):
            t_ref[c, pl.ds(b, steps, stride=nb), :] = xb[:, c * LANES:(c + 1) * LANES]
    h = jnp.concatenate([t_ref[c] for c in range(chunks)], axis=-1)
    h_ref[...] = h
    u_ref[...] = _rms(h, g_ref[...])


def norm_in(x, g, steps):
    nb, seq, d = x.shape
    rows = steps * nb
    row_spec = pl.BlockSpec((rows, d), lambda i: (i, 0))
    return pl.pallas_call(
        functools.partial(_norm_in_kernel, nb=nb, steps=steps),
        out_shape=(jax.ShapeDtypeStruct((nb * seq, d), F32), jax.ShapeDtypeStruct((nb * seq, d), F32)),
        grid=(seq // steps,),
        in_specs=[pl.BlockSpec((nb, steps, d), lambda i: (0, i, 0)),
                  pl.BlockSpec((1, d), lambda i: (0, 0))],
        out_specs=(row_spec, row_spec),
        scratch_shapes=[pltpu.VMEM((d // LANES, rows, LANES), F32)],
        compiler_params=_cparams(("parallel",)),
        name="norm_in",
    )(x, g.reshape(1, d))


def _zoh(a_re, a_im, log_dt):
    dt = jnp.exp(log_dt)
    mag = jnp.exp(a_re * dt)
    return mag * jnp.cos(a_im * dt), mag * jnp.sin(a_im * dt)


def _ssm_prep_kernel(are_ref, aim_ref, ldt_ref, bre_ref, bim_ref,
                     are2_ref, aim2_ref, ldt2_ref,
                     bbre_ref, bbim_ref, lre_ref, lim_ref):
    a_re = are_ref[...]
    a_im = aim_ref[...]
    lb_re, lb_im = _zoh(a_re, a_im, ldt_ref[...])
    den = a_re * a_re + a_im * a_im
    nr = lb_re - 1.0
    ni = lb_im
    coef_re = (nr * a_re + ni * a_im) / den
    coef_im = (ni * a_re - nr * a_im) / den
    br = bre_ref[...]
    bi = bim_ref[...]
    bbre_ref[...] = coef_re * br - coef_im * bi
    bbim_ref[...] = coef_re * bi + coef_im * br
    l_re, l_im = _zoh(are2_ref[...], aim2_ref[...], ldt2_ref[...])
    lre_ref[...] = l_re
    lim_ref[...] = l_im


def ssm_prep(a_re, a_im, log_dt, b_re, b_im):
    g, p, hh = b_re.shape
    rep = lambda a: jnp.repeat(a.astype(F32), hh, axis=1)
    ldt2 = jnp.broadcast_to(log_dt.astype(F32)[:, None], (g, p))
    outs = pl.pallas_call(
        _ssm_prep_kernel,
        out_shape=(jax.ShapeDtypeStruct((g, p * hh), F32), jax.ShapeDtypeStruct((g, p * hh), F32),
                   jax.ShapeDtypeStruct((g, p), F32), jax.ShapeDtypeStruct((g, p), F32)),
        name="ssm_prep",
    )(rep(a_re), rep(a_im), rep(ldt2), b_re.astype(F32).reshape(g, p * hh), b_im.astype(F32).reshape(g, p * hh),
      a_re.astype(F32), a_im.astype(F32), ldt2)
    bb_re, bb_im, l_re, l_im = outs
    return bb_re.reshape(g, p, hh), bb_im.reshape(g, p, hh), l_re, l_im


def _ssm_block_weights(bb_re, bb_im, c_re, c_im, l_re, l_im, gb):
    g, p, hh = bb_re.shape
    j = g // gb
    eye = jnp.eye(gb, dtype=F32)

    def emb_b(bb):
        return jnp.einsum('jgph,gk->jghkp', bb.reshape(j, gb, p, hh), eye).reshape(j, gb * hh, gb * p)

    def emb_c(c):
        return jnp.einsum('jghp,gk->jkpgh', c.reshape(j, gb, hh, p), eye).reshape(j, gb * p, gb * hh)

    wb = jnp.concatenate([emb_b(bb_re), emb_b(bb_im)], axis=2).astype(BF16)
    wc = jnp.concatenate([emb_c(c_re.astype(F32)), -emb_c(c_im.astype(F32))], axis=1).astype(BF16)
    lre = l_re.reshape(j, 1, gb * p)
    lim = l_im.reshape(j, 1, gb * p)
    return wb, wc, lre, lim


def _s5_kernel(u_ref, wb_ref, lre_ref, lim_ref, wc_ref, d_ref, o_ref,
               bu_ref, sre_ref, sim_ref, *, steps, nb, ns):
    @pl.when(pl.program_id(1) == 0)
    def _():
        sre_ref[...] = jnp.zeros_like(sre_ref)
        sim_ref[...] = jnp.zeros_like(sim_ref)

    u = u_ref[...]
    bu_ref[...] = jnp.dot(u.astype(BF16), wb_ref[...], preferred_element_type=F32)
    lr = jnp.broadcast_to(lre_ref[...], (nb, ns))
    li = jnp.broadcast_to(lim_ref[...], (nb, ns))

    def step(t, carry):
        sr, si = carry
        r0 = pl.multiple_of(t * nb, nb)
        br = bu_ref[pl.ds(r0, nb), 0:ns]
        bi = bu_ref[pl.ds(r0, nb), ns:2 * ns]
        nr = lr * sr - li * si + br
        ni = lr * si + li * sr + bi
        bu_ref[pl.ds(r0, nb), 0:ns] = nr
        bu_ref[pl.ds(r0, nb), ns:2 * ns] = ni
        return nr, ni

    sr, si = lax.fori_loop(0, steps, step, (sre_ref[...], sim_ref[...]))
    sre_ref[...] = sr
    sim_ref[...] = si
    y = jnp.dot(bu_ref[...].astype(BF16), wc_ref[...], preferred_element_type=F32)
    y = y + d_ref[...] * u
    o_ref[...] = jax.nn.gelu(y).astype(o_ref.dtype)


def s5_scan(u, wb, lre, lim, wc, d_skip, nb, steps):
    n, d = u.shape
    j, cb, ns2 = wb.shape
    ns = ns2 // 2
    rows = steps * nb
    return pl.pallas_call(
        functools.partial(_s5_kernel, steps=steps, nb=nb, ns=ns),
        out_shape=jax.ShapeDtypeStruct((n, d), BF16),
        grid=(j, n // rows),
        in_specs=[pl.BlockSpec((rows, cb), lambda jj, i: (i, jj)),
                  pl.BlockSpec((None, cb, ns2), lambda jj, i: (jj, 0, 0)),
                  pl.BlockSpec((None, 1, ns), lambda jj, i: (jj, 0, 0)),
                  pl.BlockSpec((None, 1, ns), lambda jj, i: (jj, 0, 0)),
                  pl.BlockSpec((None, ns2, cb), lambda jj, i: (jj, 0, 0)),
                  pl.BlockSpec((1, cb), lambda jj, i: (0, jj))],
        out_specs=pl.BlockSpec((rows, cb), lambda jj, i: (i, jj)),
        scratch_shapes=[pltpu.VMEM((rows, ns2), F32),
                        pltpu.VMEM((nb, ns), F32),
                        pltpu.VMEM((nb, ns), F32)],
        compiler_params=_cparams(("parallel", "arbitrary")),
        name="s5_scan",
    )(u, wb, lre, lim, wc, d_skip.reshape(1, d))


def _glu_kernel(y_ref, wa_ref, wb_ref, h_ref, o_ref):
    y = y_ref[...]
    a = jnp.dot(y, wa_ref[...], preferred_element_type=F32)
    b = jnp.dot(y, wb_ref[...], preferred_element_type=F32)
    o_ref[...] = h_ref[...] + a * jax.nn.sigmoid(b)


def glu_residual(y, wa, wb, h, rows, cols):
    n, d = y.shape
    dn = wa.shape[1]
    return pl.pallas_call(
        _glu_kernel,
        out_shape=jax.ShapeDtypeStruct((n, dn), F32),
        grid=(dn // cols, n // rows),
        in_specs=[pl.BlockSpec((rows, d), lambda c, i: (i, 0)),
                  pl.BlockSpec((d, cols), lambda c, i: (0, c)),
                  pl.BlockSpec((d, cols), lambda c, i: (0, c)),
                  pl.BlockSpec((rows, cols), lambda c, i: (i, c))],
        out_specs=pl.BlockSpec((rows, cols), lambda c, i: (i, c)),
        compiler_params=_cparams(("parallel", "parallel")),
        name="glu_residual",
    )(y, wa, wb, h)


def _mm_res_kernel(x_ref, w_ref, h_ref, o_ref):
    o_ref[...] = h_ref[...] + jnp.dot(x_ref[...], w_ref[...], preferred_element_type=F32)


def matmul_residual(x, w, h, rows, cols):
    n, d = x.shape
    dn = w.shape[1]
    return pl.pallas_call(
        _mm_res_kernel,
        out_shape=jax.ShapeDtypeStruct((n, dn), F32),
        grid=(dn // cols, n // rows),
        in_specs=[pl.BlockSpec((rows, d), lambda c, i: (i, 0)),
                  pl.BlockSpec((d, cols), lambda c, i: (0, c)),
                  pl.BlockSpec((rows, cols), lambda c, i: (i, c))],
        out_specs=pl.BlockSpec((rows, cols), lambda c, i: (i, c)),
        compiler_params=_cparams(("parallel", "parallel")),
        name="matmul_residual",
    )(x, w, h)


def _conv_in_kernel(u_ref, wbg_ref, wcg_ref, wv_ref, cw_ref, o_ref, z_ref, *, rows, nb, width):
    halo = (width - 1) * nb

    @pl.when(pl.program_id(1) == 0)
    def _():
        z_ref[0:halo, :] = jnp.zeros((halo, z_ref.shape[1]), F32)

    u = u_ref[...]
    bg = jnp.dot(u, wbg_ref[...], preferred_element_type=F32)
    cg = jnp.dot(u, wcg_ref[...], preferred_element_type=F32)
    v = jnp.dot(u, wv_ref[...], preferred_element_type=F32)
    z_ref[halo:halo + rows, :] = cg * v
    y = cw_ref[0:1, :] * z_ref[0:rows, :]
    for k in range(1, width):
        y = y + cw_ref[k:k + 1, :] * z_ref[k * nb:k * nb + rows, :]
    o_ref[...] = (bg * y).astype(o_ref.dtype)
    z_ref[0:halo, :] = z_ref[rows:rows + halo, :]


def conv_in(u, w_in, conv_w, nb, rows, cols):
    n, d = u.shape
    width = conv_w.shape[0]
    nc = d // cols
    halo = (width - 1) * nb
    return pl.pallas_call(
        functools.partial(_conv_in_kernel, rows=rows, nb=nb, width=width),
        out_shape=jax.ShapeDtypeStruct((n, d), BF16),
        grid=(nc, n // rows),
        in_specs=[pl.BlockSpec((rows, d), lambda c, i: (i, 0)),
                  pl.BlockSpec((d, cols), lambda c, i: (0, c)),
                  pl.BlockSpec((d, cols), lambda c, i: (0, c + nc)),
                  pl.BlockSpec((d, cols), lambda c, i: (0, c + 2 * nc)),
                  pl.BlockSpec((width, cols), lambda c, i: (0, c))],
        out_specs=pl.BlockSpec((rows, cols), lambda c, i: (i, c)),
        scratch_shapes=[pltpu.VMEM((rows + halo, cols), F32)],
        compiler_params=_cparams(("parallel", "arbitrary")),
        name="conv_in",
    )(u, w_in, w_in, w_in, conv_w)


def _router_kernel(h_ref, g_ref, whi_ref, wlo_ref, tri_ref, xn_ref, mf_ref, mi_ref, cnt_ref,
                   *, rows, d, n_groups, epg):
    @pl.when(pl.program_id(0) == 0)
    def _():
        cnt_ref[...] = jnp.zeros_like(cnt_ref)

    xn = _rms(h_ref[...], g_ref[...])
    chunks = d // LANES
    for c in range(chunks):
        xn_ref[pl.ds(c, rows, stride=chunks), :] = xn[:, c * LANES:(c + 1) * LANES]

    x_hi = xn.astype(BF16)
    x_lo = (xn - x_hi.astype(F32)).astype(BF16)
    w_hi = whi_ref[...]
    logits = (jnp.dot(x_hi, w_hi, preferred_element_type=F32)
              + jnp.dot(x_lo, w_hi, preferred_element_type=F32)
              + jnp.dot(x_hi, wlo_ref[...], preferred_element_type=F32))

    lane = lax.broadcasted_iota(I32, (rows, LANES), 1).astype(F32)
    neg = jnp.float32(-jnp.inf)
    big = jnp.float32(LANES)

    def first_max(vals, mask):
        m = jnp.max(jnp.where(mask, vals, neg), axis=-1, keepdims=True)
        idx = jnp.min(jnp.where(mask & (vals == m), lane, big), axis=-1, keepdims=True)
        return m, idx

    gmask = lane < n_groups
    gmax, gidx = first_max(logits, gmask)
    gsum = jnp.sum(jnp.where(gmask, jnp.exp(logits - gmax), 0.0), axis=-1, keepdims=True)
    g_w = 1.0 / gsum
    lo = n_groups + gidx * epg
    emask = (lane >= lo) & (lane < lo + epg)
    t1, i1 = first_max(logits, emask)
    t2, i2 = first_max(logits, emask & (lane != i1))
    e2 = jnp.exp(t2 - t1)
    p1 = 1.0 / (1.0 + e2)
    p2 = e2 / (1.0 + e2)
    eid1 = i1 - n_groups
    eid2 = i2 - n_groups

    onehot = ((lane == eid1) | (lane == eid2))
    oh = jnp.where(onehot, 1.0, 0.0).astype(BF16)
    before = jnp.dot(tri_ref[...], oh, preferred_element_type=F32) + cnt_ref[...]
    r1 = jnp.sum(jnp.where(lane == eid1, before, 0.0), axis=-1, keepdims=True)
    r2 = jnp.sum(jnp.where(lane == eid2, before, 0.0), axis=-1, keepdims=True)
    cnt_ref[...] = cnt_ref[...] + jnp.sum(oh.astype(F32), axis=0, keepdims=True)

    mf_ref[...] = jnp.where(lane == 0, g_w * p1, jnp.where(lane == 1, g_w * p2, 0.0))
    mi_ref[...] = jnp.where(lane == 0, eid1, jnp.where(lane == 1, eid2,
                            jnp.where(lane == 2, r1, jnp.where(lane == 3, r2, 0.0)))).astype(I32)


def router(h, g, w_group, w_expert, rows):
    n, d = h.shape
    n_groups = w_group.shape[1]
    epg = w_expert.shape[2]
    n_exp = n_groups * epg
    assert n_groups + n_exp <= LANES
    chunks = d // LANES
    w = jnp.concatenate([w_group.astype(F32),
                         jnp.transpose(w_expert.astype(F32), (1, 0, 2)).reshape(d, n_exp)], axis=1)
    w = jnp.pad(w, ((0, 0), (0, LANES - n_groups - n_exp)))
    w_hi = w.astype(BF16)
    w_lo = (w - w_hi.astype(F32)).astype(BF16)
    tri = (lax.broadcasted_iota(I32, (rows, rows), 1) < lax.broadcasted_iota(I32, (rows, rows), 0)).astype(BF16)
    xn, mf, mi, cnt = pl.pallas_call(
        functools.partial(_router_kernel, rows=rows, d=d, n_groups=n_groups, epg=epg),
        out_shape=(jax.ShapeDtypeStruct((n * chunks, LANES), F32),
                   jax.ShapeDtypeStruct((n, LANES), F32),
                   jax.ShapeDtypeStruct((n, LANES), I32),
                   jax.ShapeDtypeStruct((1, LANES), F32)),
        grid=(n // rows,),
        in_specs=[pl.BlockSpec((rows, d), lambda i: (i, 0)),
                  pl.BlockSpec((1, d), lambda i: (0, 0)),
                  pl.BlockSpec((d, LANES), lambda i: (0, 0)),
                  pl.BlockSpec((d, LANES), lambda i: (0, 0)),
                  pl.BlockSpec((rows, rows), lambda i: (0, 0))],
        out_specs=(pl.BlockSpec((rows * chunks, LANES), lambda i: (i, 0)),
                   pl.BlockSpec((rows, LANES), lambda i: (i, 0)),
                   pl.BlockSpec((rows, LANES), lambda i: (i, 0)),
                   pl.BlockSpec((1, LANES), lambda i: (0, 0))),
        compiler_params=_cparams(("arbitrary",)),
        name="moe_router",
    )(h, g.reshape(1, d), w_hi, w_lo, tri)
    return xn, mf, mi, cnt


def _dispatch_kernel(pstart_ref, er_ref, xn_ref, xs_in_ref, xs_ref, sem, *, toks, chunks):
    del xs_in_ref

    def row_copy(r, slot):
        e = er_ref[0, 0, 4 * r + slot]
        dst = pstart_ref[e] + er_ref[0, 0, 4 * r + 2 + slot]
        return pltpu.make_async_copy(xn_ref.at[pl.ds(pl.multiple_of(r * chunks, chunks), chunks)],
                                     xs_ref.at[pl.ds(pl.multiple_of(dst * chunks, chunks), chunks)], sem)

    def issue(r, c):
        row_copy(r, 0).start()
        row_copy(r, 1).start()
        return c

    lax.fori_loop(0, toks, issue, 0)

    def drain(r, c):
        row_copy(r, 0).wait()
        row_copy(r, 1).wait()
        return c

    lax.fori_loop(0, toks, drain, 0)


def moe_dispatch(pad_start, er, xn_lin, xs_init, toks, chunks):
    n = xn_lin.shape[0] // chunks
    return pl.pallas_call(
        functools.partial(_dispatch_kernel, toks=toks, chunks=chunks),
        out_shape=jax.ShapeDtypeStruct(xs_init.shape, F32),
        grid_spec=pltpu.PrefetchScalarGridSpec(
            num_scalar_prefetch=1,
            grid=(n // toks,),
            in_specs=[pl.BlockSpec((1, 1, 4 * toks), lambda i, ps: (i, 0, 0), memory_space=pltpu.SMEM),
                      pl.BlockSpec((toks * chunks, LANES), lambda i, ps: (i, 0)),
                      pl.BlockSpec(memory_space=pl.ANY)],
            out_specs=pl.BlockSpec(memory_space=pl.ANY),
            scratch_shapes=[pltpu.SemaphoreType.DMA(())]),
        input_output_aliases={3: 0},
        compiler_params=_cparams(("arbitrary",)),
        name="moe_dispatch",
    )(pad_start, er, xn_lin, xs_init)


def _moe_ffn_kernel(be_ref, nu_ref, first_ref, nxt_ref, par_ref, xs_ref, wg_hbm, wu_hbm, wd_hbm, ys_ref,
                    wgs_ref, wus_ref, wds_ref, wgb_ref, wub_ref, wdb_ref, sem, *, layer, bm, chunks):
    i = pl.program_id(0)

    def weight_copies(e, s):
        return (pltpu.make_async_copy(wg_hbm.at[layer, e], wgs_ref.at[s], sem.at[s, 0]),
                pltpu.make_async_copy(wu_hbm.at[layer, e], wus_ref.at[s], sem.at[s, 1]),
                pltpu.make_async_copy(wd_hbm.at[layer, e], wds_ref.at[s], sem.at[s, 2]))

    @pl.when(i == 0)
    def _():
        for cp in weight_copies(be_ref[0], 0):
            cp.start()

    for s in range(2):
        @pl.when((first_ref[i] == 1) & (par_ref[i] == s))
        def _():
            for cp in weight_copies(be_ref[i], s):
                cp.wait()

            @pl.when(nxt_ref[i] >= 0)
            def _():
                for cp in weight_copies(nxt_ref[i], 1 - s):
                    cp.start()

            wgb_ref[...] = wgs_ref[s].astype(BF16)
            wub_ref[...] = wus_ref[s].astype(BF16)
            wdb_ref[...] = wds_ref[s].astype(BF16)

    @pl.when(i < nu_ref[0])
    def _():
        x = jnp.concatenate([xs_ref[pl.ds(c, bm, stride=chunks), :] for c in range(chunks)], axis=-1).astype(BF16)
        gt = jnp.dot(x, wgb_ref[...], preferred_element_type=F32)
        up = jnp.dot(x, wub_ref[...], preferred_element_type=F32)
        mid = (jax.nn.silu(gt) * up).astype(BF16)
        y = jnp.dot(mid, wdb_ref[...], preferred_element_type=F32)
        for c in range(chunks):
            ys_ref[pl.ds(c, bm, stride=chunks), :] = y[:, c * LANES:(c + 1) * LANES]

    @pl.when(i >= nu_ref[0])
    def _():
        ys_ref[...] = jnp.zeros_like(ys_ref)


def moe_ffn(blk_expert, n_used, xs_lin, w_gate, w_up, w_down, layer, bm, chunks):
    n_blocks = xs_lin.shape[0] // (bm * chunks)
    _, _, d, de = w_gate.shape

    def row_map(i, be, nu, *_):
        return (jnp.minimum(i, nu[0] - 1), 0)

    ar = jnp.arange(n_blocks, dtype=I32)
    first = jnp.concatenate([jnp.ones((1,), I32), (blk_expert[1:] != blk_expert[:-1]).astype(I32)])
    parity = (jnp.cumsum(first) - 1) % 2
    start_pos = jnp.where(first == 1, ar, n_blocks)
    next_start = jnp.concatenate([lax.cummin(start_pos, reverse=True)[1:], jnp.full((1,), n_blocks, I32)])
    nxt = jnp.where(next_start < n_blocks, blk_expert[jnp.minimum(next_start, n_blocks - 1)], -1).astype(I32)

    any_spec = pl.BlockSpec(memory_space=pl.ANY)
    return pl.pallas_call(
        functools.partial(_moe_ffn_kernel, layer=layer, bm=bm, chunks=chunks),
        out_shape=jax.ShapeDtypeStruct(xs_lin.shape, F32),
        grid_spec=pltpu.PrefetchScalarGridSpec(
            num_scalar_prefetch=5,
            grid=(n_blocks,),
            in_specs=[pl.BlockSpec((bm * chunks, LANES), row_map), any_spec, any_spec, any_spec],
            out_specs=pl.BlockSpec((bm * chunks, LANES), lambda i, *_: (i, 0)),
            scratch_shapes=[pltpu.VMEM((2, d, de), F32), pltpu.VMEM((2, d, de), F32), pltpu.VMEM((2, de, d), F32),
                            pltpu.VMEM((d, de), BF16), pltpu.VMEM((d, de), BF16), pltpu.VMEM((de, d), BF16),
                            pltpu.SemaphoreType.DMA((2, 3))]),
        compiler_params=_cparams(("arbitrary",)),
        name="moe_ffn",
    )(blk_expert, n_used, first, nxt, parity.astype(I32), xs_lin, w_gate, w_up, w_down)


def _slab_pitch(chunks):
    return chunks + 2


def _combine_kernel(pstart_ref, erc_ref, ern_ref, ys_ref, h_ref, mf_ref, g_ref, *rest, toks, chunks, n_tiles, nb):
    if nb:
        fo_ref, buf_ref, ub_ref, sem = rest
    else:
        ho_ref, uo_ref, buf_ref, sem = rest
    i = pl.program_id(0)
    slab = _slab_pitch(chunks)
    slot_rows = toks * TOP_K * slab
    stride = TOP_K * slab

    def buf_rows(slot, r, k):
        return buf_ref.at[pl.ds(slot * slot_rows + (r * TOP_K + k) * slab, chunks)]

    def start_tile(er_ref, slot):
        def body(r, c):
            for k in range(TOP_K):
                src = pstart_ref[er_ref[0, 0, 4 * r + k]] + er_ref[0, 0, 4 * r + 2 + k]
                pltpu.make_async_copy(ys_ref.at[pl.ds(pl.multiple_of(src * chunks, chunks), chunks)],
                                      buf_rows(slot, r, k), sem.at[slot]).start()
            return c

        lax.fori_loop(0, toks, body, 0)

    def wait_tile(slot):
        def body(r, c):
            for k in range(TOP_K):
                pltpu.make_async_copy(ys_ref.at[pl.ds(0, chunks)], buf_rows(slot, r, k), sem.at[slot]).wait()
            return c

        lax.fori_loop(0, toks, body, 0)

    def run(slot):
        if slot == 0:
            @pl.when(i == 0)
            def _():
                start_tile(erc_ref, 0)

        @pl.when(i + 1 < n_tiles)
        def _():
            start_tile(ern_ref, 1 - slot)

        wait_tile(slot)
        base = slot * slot_rows
        y0 = jnp.concatenate([buf_ref[pl.ds(base + c, toks, stride=stride), :] for c in range(chunks)], axis=-1)
        y1 = jnp.concatenate([buf_ref[pl.ds(base + slab + c, toks, stride=stride), :] for c in range(chunks)],
                             axis=-1)
        mf = mf_ref[...]
        h = h_ref[...] + mf[:, 0:1] * y0 + mf[:, 1:2] * y1
        if nb:
            u = _rms(h, g_ref[...])
            for c in range(chunks):
                ub_ref[c] = u[:, c * LANES:(c + 1) * LANES]
            for b in range(nb):
                fo_ref[b] = jnp.concatenate(
                    [ub_ref[c, pl.ds(b, toks // nb, stride=nb), :] for c in range(chunks)], axis=-1)
        else:
            ho_ref[...] = h
            uo_ref[...] = _rms(h, g_ref[...]).astype(uo_ref.dtype)

    @pl.when(i % 2 == 0)
    def _():
        run(0)

    @pl.when(i % 2 == 1)
    def _():
        run(1)


def moe_combine(pad_start, er_all, ys_lin, h, mf, g_next, u_dtype, toks, chunks, nb=0):
    n, d = h.shape
    n_tiles = n // toks
    row_spec = pl.BlockSpec((toks, d), lambda i, ps: (i, 0))
    gather_buf = pltpu.VMEM((2 * toks * TOP_K * _slab_pitch(chunks), LANES), F32)
    if nb:
        out_shape = jax.ShapeDtypeStruct((nb, n // nb, d), F32)
        out_specs = pl.BlockSpec((nb, toks // nb, d), lambda i, ps: (0, i, 0))
        scratch = [gather_buf, pltpu.VMEM((chunks, toks, LANES), F32), pltpu.SemaphoreType.DMA((2,))]
    else:
        out_shape = (jax.ShapeDtypeStruct((n, d), F32), jax.ShapeDtypeStruct((n, d), u_dtype))
        out_specs = (row_spec, row_spec)
        scratch = [gather_buf, pltpu.SemaphoreType.DMA((2,))]
    return pl.pallas_call(
        functools.partial(_combine_kernel, toks=toks, chunks=chunks, n_tiles=n_tiles, nb=nb),
        out_shape=out_shape,
        grid_spec=pltpu.PrefetchScalarGridSpec(
            num_scalar_prefetch=1,
            grid=(n_tiles,),
            in_specs=[pl.BlockSpec((1, 1, 4 * toks), lambda i, ps: (i, 0, 0), memory_space=pltpu.SMEM),
                      pl.BlockSpec((1, 1, 4 * toks), lambda i, ps: (jnp.minimum(i + 1, n_tiles - 1), 0, 0),
                                   memory_space=pltpu.SMEM),
                      pl.BlockSpec(memory_space=pl.ANY),
                      row_spec,
                      pl.BlockSpec((toks, LANES), lambda i, ps: (i, 0)),
                      pl.BlockSpec((1, d), lambda i, ps: (0, 0))],
            out_specs=out_specs,
            scratch_shapes=scratch),
        compiler_params=_cparams(("arbitrary",)),
        name="moe_combine",
    )(pad_start, er_all, er_all, ys_lin, h, mf, g_next.reshape(1, d))


def hier_moe_layer(h, g_ffn, w_group, w_expert, w_gate, w_up, w_down, layer, g_next, u_dtype, row_buf,
                   *, rows, bm, toks_d, toks_c, final_nb=0):
    n, d = h.shape
    chunks = d // LANES
    n_exp = w_gate.shape[1]
    xn_lin, mf, mi, cnt = router(h, g_ffn, w_group, w_expert, rows)

    counts = cnt[0, :n_exp].astype(I32)
    padded = (counts + bm - 1) // bm * bm
    pad_end = jnp.cumsum(padded)
    pad_start = pad_end - padded
    n_rows = n * TOP_K + n_exp * bm
    n_blocks = n_rows // bm
    blk_start = jnp.arange(n_blocks, dtype=I32) * bm
    blk_expert = jnp.sum((blk_start[:, None] >= pad_end[None, :]).astype(I32), axis=1)
    blk_expert = jnp.minimum(blk_expert, n_exp - 1).astype(I32)
    n_used = (pad_end[-1:] // bm).astype(I32)
    last_e = blk_expert[jnp.maximum(n_used[0] - 1, 0)]
    blk_expert = jnp.where(jnp.arange(n_blocks) < n_used[0], blk_expert, last_e)

    er = mi[:, :4]
    if row_buf is None:
        row_buf = jnp.zeros((n_rows * chunks, LANES), F32)
    xs_lin = moe_dispatch(pad_start, er.reshape(n // toks_d, 1, 4 * toks_d), xn_lin, row_buf, toks_d, chunks)
    ys_lin = moe_ffn(blk_expert, n_used, xs_lin, w_gate, w_up, w_down, layer, bm, chunks)
    er_c = er.reshape(n // toks_c, 1, 4 * toks_c)
    if final_nb:
        return None, moe_combine(pad_start, er_c, ys_lin, h, mf, g_next, u_dtype, toks_c, chunks, final_nb), None
    h_out, u_out = moe_combine(pad_start, er_c, ys_lin, h, mf, g_next, u_dtype, toks_c, chunks)
    return h_out, u_out, ys_lin


def trunk(x, norm_mix, norm_ffn, norm_final, ssm_a_re, ssm_a_im, ssm_log_dt, ssm_b_re, ssm_b_im,
          ssm_c_re, ssm_c_im, ssm_d, ssm_w_glu_a, ssm_w_glu_b, conv_w_in, conv_w, conv_w_out,
          moe_w_group_router, moe_w_expert_router, moe_w_gate, moe_w_up, moe_w_down,
          *, rows, steps, cols, bm, toks_d, toks_c, cblock):
    bsz, seq, d = x.shape
    depth = norm_mix.shape[0]
    n = bsz * seq
    hh = ssm_b_re.shape[3]
    gb = cblock // hh

    h, u = norm_in(x, norm_mix[0], steps)
    row_buf = None
    for i in range(depth):
        j = i // 2
        if i % 2 == 0:
            bb_re, bb_im, l_re, l_im = ssm_prep(ssm_a_re[j], ssm_a_im[j], ssm_log_dt[j], ssm_b_re[j], ssm_b_im[j])
            wb, wc, lre, lim = _ssm_block_weights(bb_re, bb_im, ssm_c_re[j], ssm_c_im[j], l_re, l_im, gb)
            yg = s5_scan(u, wb, lre, lim, wc, ssm_d[j].astype(F32), bsz, steps)
            h = glu_residual(yg, ssm_w_glu_a[j].astype(BF16), ssm_w_glu_b[j].astype(BF16), h, rows, cols)
        else:
            gy = conv_in(u, conv_w_in[j].astype(BF16), conv_w[j].astype(F32), bsz, rows, cols)
            h = matmul_residual(gy, conv_w_out[j].astype(BF16), h, rows, cols)
        last = i == depth - 1
        g_next = norm_final if last else norm_mix[i + 1]
        u_dtype = F32 if (last or (i + 1) % 2 == 0) else BF16
        h, u, row_buf = hier_moe_layer(h, norm_ffn[i], moe_w_group_router[i], moe_w_expert_router[i],
                                       moe_w_gate, moe_w_up, moe_w_down, i, g_next, u_dtype, row_buf,
                                       rows=rows, bm=bm, toks_d=toks_d, toks_c=toks_c,
                                       final_nb=bsz if last else 0)
    return u


def kernel(x, norm_mix, norm_ffn, norm_final, ssm_a_re, ssm_a_im, ssm_log_dt, ssm_b_re, ssm_b_im, ssm_c_re, ssm_c_im, ssm_d, ssm_w_glu_a, ssm_w_glu_b, conv_w_in, conv_w, conv_w_out, moe_w_group_router, moe_w_expert_router, moe_w_gate, moe_w_up, moe_w_down):
    return trunk(x, norm_mix, norm_ffn, norm_final, ssm_a_re, ssm_a_im, ssm_log_dt, ssm_b_re, ssm_b_im,
                 ssm_c_re, ssm_c_im, ssm_d, ssm_w_glu_a, ssm_w_glu_b, conv_w_in, conv_w, conv_w_out,
                 moe_w_group_router, moe_w_expert_router, moe_w_gate, moe_w_up, moe_w_down,
                 rows=512, steps=64, cols=512, bm=256, toks_d=512, toks_c=128, cblock=256)
```

```python
import functools

import jax
import jax.numpy as jnp
from jax import lax
from jax.experimental import pallas as pl
from jax.experimental.pallas import tpu as pltpu

F32 = jnp.float32
BF16 = jnp.bfloat16
I32 = jnp.int32

LANES = 128
SUBLANES = 8
VMEM_LIMIT = 56 * 1024 * 1024
RMS_EPS = 1e-6
TOP_K = 2
DMA_UNROLL = 8


def _cparams(sem):
    return pltpu.CompilerParams(dimension_semantics=sem, vmem_limit_bytes=VMEM_LIMIT)


def _rms(h, g):
    ms = jnp.mean(h * h, axis=-1, keepdims=True)
    return h * lax.rsqrt(ms + RMS_EPS) * g


def _norm_in_kernel(x_ref, g_ref, h_ref, u_ref, t_ref, *, nb, steps):
    chunks = t_ref.shape[0]
    for b in range(nb):
        xb = x_ref[b]
        for c in range(chunks):
            t_ref[c, pl.ds(b, steps, stride=nb), :] = xb[:, c * LANES:(c + 1) * LANES]
    h = jnp.concatenate([t_ref[c] for c in range(chunks)], axis=-1)
    h_ref[...] = h
    u_ref[...] = _rms(h, g_ref[...])


def norm_in(x, g, steps):
    nb, seq, d = x.shape
    rows = steps * nb
    row_spec = pl.BlockSpec((rows, d), lambda i: (i, 0))
    return pl.pallas_call(
        functools.partial(_norm_in_kernel, nb=nb, steps=steps),
        out_shape=(jax.ShapeDtypeStruct((nb * seq, d), F32), jax.ShapeDtypeStruct((nb * seq, d), F32)),
        grid=(seq // steps,),
        in_specs=[pl.BlockSpec((nb, steps, d), lambda i: (0, i, 0)),
                  pl.BlockSpec((1, d), lambda i: (0, 0))],
        out_specs=(row_spec, row_spec),
        scratch_shapes=[pltpu.VMEM((d // LANES, rows, LANES), F32)],
        compiler_params=_cparams(("parallel",)),
        name="norm_in",
    )(x, g.reshape(1, d))


def _zoh(a_re, a_im, log_dt):
    dt = jnp.exp(log_dt)
    mag = jnp.exp(a_re * dt)
    return mag * jnp.cos(a_im * dt), mag * jnp.sin(a_im * dt)


def _ssm_prep_kernel(are_ref, aim_ref, ldt_ref, bre_ref, bim_ref,
                     are2_ref, aim2_ref, ldt2_ref,
                     bbre_ref, bbim_ref, lre_ref, lim_ref):
    a_re = are_ref[...]
    a_im = aim_ref[...]
    lb_re, lb_im = _zoh(a_re, a_im, ldt_ref[...])
    den = a_re * a_re + a_im * a_im
    nr = lb_re - 1.0
    ni = lb_im
    coef_re = (nr * a_re + ni * a_im) / den
    coef_im = (ni * a_re - nr * a_im) / den
    br = bre_ref[...]
    bi = bim_ref[...]
    bbre_ref[...] = coef_re * br - coef_im * bi
    bbim_ref[...] = coef_re * bi + coef_im * br
    l_re, l_im = _zoh(are2_ref[...], aim2_ref[...], ldt2_ref[...])
    lre_ref[...] = l_re
    lim_ref[...] = l_im


def ssm_prep(a_re, a_im, log_dt, b_re, b_im):
    g, p, hh = b_re.shape
    rep = lambda a: jnp.repeat(a.astype(F32), hh, axis=1)
    ldt2 = jnp.broadcast_to(log_dt.astype(F32)[:, None], (g, p))
    outs = pl.pallas_call(
        _ssm_prep_kernel,
        out_shape=(jax.ShapeDtypeStruct((g, p * hh), F32), jax.ShapeDtypeStruct((g, p * hh), F32),
                   jax.ShapeDtypeStruct((g, p), F32), jax.ShapeDtypeStruct((g, p), F32)),
        name="ssm_prep",
    )(rep(a_re), rep(a_im), rep(ldt2), b_re.astype(F32).reshape(g, p * hh), b_im.astype(F32).reshape(g, p * hh),
      a_re.astype(F32), a_im.astype(F32), ldt2)
    bb_re, bb_im, l_re, l_im = outs
    return bb_re.reshape(g, p, hh), bb_im.reshape(g, p, hh), l_re, l_im


def _ssm_block_weights(bb_re, bb_im, c_re, c_im, l_re, l_im, gb):
    g, p, hh = bb_re.shape
    j = g // gb
    eye = jnp.eye(gb, dtype=F32)

    def emb_b(bb):
        return jnp.einsum('jgph,gk->jghkp', bb.reshape(j, gb, p, hh), eye).reshape(j, gb * hh, gb * p)

    def emb_c(c):
        return jnp.einsum('jghp,gk->jkpgh', c.reshape(j, gb, hh, p), eye).reshape(j, gb * p, gb * hh)

    wb = jnp.concatenate([emb_b(bb_re), emb_b(bb_im)], axis=2).astype(BF16)
    wc = jnp.concatenate([emb_c(c_re.astype(F32)), -emb_c(c_im.astype(F32))], axis=1).astype(BF16)
    lre = l_re.reshape(j, 1, gb * p)
    lim = l_im.reshape(j, 1, gb * p)
    return wb, wc, lre, lim


def _s5_kernel(u_ref, wb_ref, lre_ref, lim_ref, wc_ref, d_ref, o_ref,
               bu_ref, sre_ref, sim_ref, *, steps, nb, ns, cb, parts):
    @pl.when(pl.program_id(1) == 0)
    def _():
        sre_ref[...] = jnp.zeros_like(sre_ref)
        sim_ref[...] = jnp.zeros_like(sim_ref)

    for p in range(parts):
        u = u_ref[:, p * cb:(p + 1) * cb]
        bu_ref[p] = jnp.dot(u.astype(BF16), wb_ref[p], preferred_element_type=F32)

    for p in range(parts):
        lr = jnp.broadcast_to(lre_ref[p], (nb, ns))
        li = jnp.broadcast_to(lim_ref[p], (nb, ns))
        sr = sre_ref[p]
        si = sim_ref[p]
        for t in range(steps):
            br = bu_ref[p, t * nb:(t + 1) * nb, 0:ns]
            bi = bu_ref[p, t * nb:(t + 1) * nb, ns:2 * ns]
            sr, si = lr * sr - li * si + br, lr * si + li * sr + bi
            bu_ref[p, t * nb:(t + 1) * nb, 0:ns] = sr
            bu_ref[p, t * nb:(t + 1) * nb, ns:2 * ns] = si
        sre_ref[p] = sr
        sim_ref[p] = si
        y = jnp.dot(bu_ref[p].astype(BF16), wc_ref[p], preferred_element_type=F32)
        y = y + d_ref[:, p * cb:(p + 1) * cb] * u_ref[:, p * cb:(p + 1) * cb]
        o_ref[:, p * cb:(p + 1) * cb] = jax.nn.gelu(y).astype(o_ref.dtype)


def s5_scan(u, wb, lre, lim, wc, d_skip, nb, steps, parts):
    n, d = u.shape
    j, cb, ns2 = wb.shape
    ns = ns2 // 2
    rows = steps * nb
    jp = j // parts
    wb = wb.reshape(jp, parts, cb, ns2)
    wc = wc.reshape(jp, parts, ns2, cb)
    lre = lre.reshape(jp, parts, 1, ns)
    lim = lim.reshape(jp, parts, 1, ns)
    return pl.pallas_call(
        functools.partial(_s5_kernel, steps=steps, nb=nb, ns=ns, cb=cb, parts=parts),
        out_shape=jax.ShapeDtypeStruct((n, d), BF16),
        grid=(jp, n // rows),
        in_specs=[pl.BlockSpec((rows, parts * cb), lambda jj, i: (i, jj)),
                  pl.BlockSpec((None, parts, cb, ns2), lambda jj, i: (jj, 0, 0, 0)),
                  pl.BlockSpec((None, parts, 1, ns), lambda jj, i: (jj, 0, 0, 0)),
                  pl.BlockSpec((None, parts, 1, ns), lambda jj, i: (jj, 0, 0, 0)),
                  pl.BlockSpec((None, parts, ns2, cb), lambda jj, i: (jj, 0, 0, 0)),
                  pl.BlockSpec((1, parts * cb), lambda jj, i: (0, jj))],
        out_specs=pl.BlockSpec((rows, parts * cb), lambda jj, i: (i, jj)),
        scratch_shapes=[pltpu.VMEM((parts, rows, ns2), F32),
                        pltpu.VMEM((parts, nb, ns), F32),
                        pltpu.VMEM((parts, nb, ns), F32)],
        compiler_params=_cparams(("parallel", "arbitrary")),
        name="s5_scan",
    )(u, wb, lre, lim, wc, d_skip.reshape(1, d))


def _glu_kernel(y_ref, wa_ref, wb_ref, h_ref, o_ref):
    y = y_ref[...]
    a = jnp.dot(y, wa_ref[...], preferred_element_type=F32)
    b = jnp.dot(y, wb_ref[...], preferred_element_type=F32)
    o_ref[...] = h_ref[...] + a * jax.nn.sigmoid(b)


def glu_residual(y, wa, wb, h, rows, cols):
    n, d = y.shape
    dn = wa.shape[1]
    return pl.pallas_call(
        _glu_kernel,
        out_shape=jax.ShapeDtypeStruct((n, dn), F32),
        grid=(dn // cols, n // rows),
        in_specs=[pl.BlockSpec((rows, d), lambda c, i: (i, 0)),
                  pl.BlockSpec((d, cols), lambda c, i: (0, c)),
                  pl.BlockSpec((d, cols), lambda c, i: (0, c)),
                  pl.BlockSpec((rows, cols), lambda c, i: (i, c))],
        out_specs=pl.BlockSpec((rows, cols), lambda c, i: (i, c)),
        compiler_params=_cparams(("parallel", "parallel")),
        name="glu_residual",
    )(y, wa, wb, h)


def _mm_res_kernel(x_ref, w_ref, h_ref, o_ref):
    o_ref[...] = h_ref[...] + jnp.dot(x_ref[...], w_ref[...], preferred_element_type=F32)


def matmul_residual(x, w, h, rows, cols):
    n, d = x.shape
    dn = w.shape[1]
    return pl.pallas_call(
        _mm_res_kernel,
        out_shape=jax.ShapeDtypeStruct((n, dn), F32),
        grid=(dn // cols, n // rows),
        in_specs=[pl.BlockSpec((rows, d), lambda c, i: (i, 0)),
                  pl.BlockSpec((d, cols), lambda c, i: (0, c)),
                  pl.BlockSpec((rows, cols), lambda c, i: (i, c))],
        out_specs=pl.BlockSpec((rows, cols), lambda c, i: (i, c)),
        compiler_params=_cparams(("parallel", "parallel")),
        name="matmul_residual",
    )(x, w, h)


def _conv_in_kernel(u_ref, wbg_ref, wcg_ref, wv_ref, cw_ref, o_ref, z_ref, *, rows, nb, width):
    halo = (width - 1) * nb

    @pl.when(pl.program_id(1) == 0)
    def _():
        z_ref[0:halo, :] = jnp.zeros((halo, z_ref.shape[1]), F32)

    u = u_ref[...]
    bg = jnp.dot(u, wbg_ref[...], preferred_element_type=F32)
    cg = jnp.dot(u, wcg_ref[...], preferred_element_type=F32)
    v = jnp.dot(u, wv_ref[...], preferred_element_type=F32)
    z_ref[halo:halo + rows, :] = cg * v
    y = cw_ref[0:1, :] * z_ref[0:rows, :]
    for k in range(1, width):
        y = y + cw_ref[k:k + 1, :] * z_ref[k * nb:k * nb + rows, :]
    o_ref[...] = (bg * y).astype(o_ref.dtype)
    z_ref[0:halo, :] = z_ref[rows:rows + halo, :]


def conv_in(u, w_in, conv_w, nb, rows, cols):
    n, d = u.shape
    width = conv_w.shape[0]
    nc = d // cols
    halo = (width - 1) * nb
    return pl.pallas_call(
        functools.partial(_conv_in_kernel, rows=rows, nb=nb, width=width),
        out_shape=jax.ShapeDtypeStruct((n, d), BF16),
        grid=(nc, n // rows),
        in_specs=[pl.BlockSpec((rows, d), lambda c, i: (i, 0)),
                  pl.BlockSpec((d, cols), lambda c, i: (0, c)),
                  pl.BlockSpec((d, cols), lambda c, i: (0, c + nc)),
                  pl.BlockSpec((d, cols), lambda c, i: (0, c + 2 * nc)),
                  pl.BlockSpec((width, cols), lambda c, i: (0, c))],
        out_specs=pl.BlockSpec((rows, cols), lambda c, i: (i, c)),
        scratch_shapes=[pltpu.VMEM((rows + halo, cols), F32)],
        compiler_params=_cparams(("parallel", "arbitrary")),
        name="conv_in",
    )(u, w_in, w_in, w_in, conv_w)


def _router_kernel(h_ref, g_ref, whi_ref, wlo_ref, tri_ref, xn_ref, mf_ref, mi_ref, cnt_ref,
                   *, rows, d, n_groups, epg):
    @pl.when(pl.program_id(0) == 0)
    def _():
        cnt_ref[...] = jnp.zeros_like(cnt_ref)

    xn = _rms(h_ref[...], g_ref[...])
    chunks = d // LANES
    for c in range(chunks):
        xn_ref[pl.ds(c, rows, stride=chunks), :] = xn[:, c * LANES:(c + 1) * LANES]

    x_hi = xn.astype(BF16)
    x_lo = (xn - x_hi.astype(F32)).astype(BF16)
    w_hi = whi_ref[...]
    logits = (jnp.dot(x_hi, w_hi, preferred_element_type=F32)
              + jnp.dot(x_lo, w_hi, preferred_element_type=F32)
              + jnp.dot(x_hi, wlo_ref[...], preferred_element_type=F32))

    lane = lax.broadcasted_iota(I32, (rows, LANES), 1).astype(F32)
    neg = jnp.float32(-jnp.inf)
    big = jnp.float32(LANES)

    def first_max(vals, mask):
        m = jnp.max(jnp.where(mask, vals, neg), axis=-1, keepdims=True)
        idx = jnp.min(jnp.where(mask & (vals == m), lane, big), axis=-1, keepdims=True)
        return m, idx

    gmask = lane < n_groups
    gmax, gidx = first_max(logits, gmask)
    gsum = jnp.sum(jnp.where(gmask, jnp.exp(logits - gmax), 0.0), axis=-1, keepdims=True)
    g_w = 1.0 / gsum
    lo = n_groups + gidx * epg
    emask = (lane >= lo) & (lane < lo + epg)
    t1, i1 = first_max(logits, emask)
    t2, i2 = first_max(logits, emask & (lane != i1))
    e2 = jnp.exp(t2 - t1)
    p1 = 1.0 / (1.0 + e2)
    p2 = e2 / (1.0 + e2)
    eid1 = i1 - n_groups
    eid2 = i2 - n_groups

    onehot = ((lane == eid1) | (lane == eid2))
    oh = jnp.where(onehot, 1.0, 0.0).astype(BF16)
    before = jnp.dot(tri_ref[...], oh, preferred_element_type=F32) + cnt_ref[...]
    r1 = jnp.sum(jnp.where(lane == eid1, before, 0.0), axis=-1, keepdims=True)
    r2 = jnp.sum(jnp.where(lane == eid2, before, 0.0), axis=-1, keepdims=True)
    cnt_ref[...] = cnt_ref[...] + jnp.sum(oh.astype(F32), axis=0, keepdims=True)

    mf_ref[...] = jnp.where(lane == 0, g_w * p1, jnp.where(lane == 1, g_w * p2, 0.0))
    mi_ref[...] = jnp.where(lane == 0, eid1, jnp.where(lane == 1, eid2,
                            jnp.where(lane == 2, r1, jnp.where(lane == 3, r2, 0.0)))).astype(I32)


def router(h, g, w_group, w_expert, rows):
    n, d = h.shape
    n_groups = w_group.shape[1]
    epg = w_expert.shape[2]
    n_exp = n_groups * epg
    assert n_groups + n_exp <= LANES
    chunks = d // LANES
    w = jnp.concatenate([w_group.astype(F32),
                         jnp.transpose(w_expert.astype(F32), (1, 0, 2)).reshape(d, n_exp)], axis=1)
    w = jnp.pad(w, ((0, 0), (0, LANES - n_groups - n_exp)))
    w_hi = w.astype(BF16)
    w_lo = (w - w_hi.astype(F32)).astype(BF16)
    tri = (lax.broadcasted_iota(I32, (rows, rows), 1) < lax.broadcasted_iota(I32, (rows, rows), 0)).astype(BF16)
    xn, mf, mi, cnt = pl.pallas_call(
        functools.partial(_router_kernel, rows=rows, d=d, n_groups=n_groups, epg=epg),
        out_shape=(jax.ShapeDtypeStruct((n * chunks, LANES), F32),
                   jax.ShapeDtypeStruct((n, LANES), F32),
                   jax.ShapeDtypeStruct((n, LANES), I32),
                   jax.ShapeDtypeStruct((1, LANES), F32)),
        grid=(n // rows,),
        in_specs=[pl.BlockSpec((rows, d), lambda i: (i, 0)),
                  pl.BlockSpec((1, d), lambda i: (0, 0)),
                  pl.BlockSpec((d, LANES), lambda i: (0, 0)),
                  pl.BlockSpec((d, LANES), lambda i: (0, 0)),
                  pl.BlockSpec((rows, rows), lambda i: (0, 0))],
        out_specs=(pl.BlockSpec((rows * chunks, LANES), lambda i: (i, 0)),
                   pl.BlockSpec((rows, LANES), lambda i: (i, 0)),
                   pl.BlockSpec((rows, LANES), lambda i: (i, 0)),
                   pl.BlockSpec((1, LANES), lambda i: (0, 0))),
        compiler_params=_cparams(("arbitrary",)),
        name="moe_router",
    )(h, g.reshape(1, d), w_hi, w_lo, tri)
    return xn, mf, mi, cnt


def _dispatch_kernel(dest_ref, xn_ref, xs_in_ref, xs_ref, sem, *, toks, chunks):
    del xs_in_ref

    def issue(r, c):
        src = xn_ref.at[pl.ds(pl.multiple_of(r * chunks, chunks), chunks)]
        for k in range(TOP_K):
            dst = dest_ref[0, 0, TOP_K * r + k]
            pltpu.make_async_copy(src, xs_ref.at[pl.ds(pl.multiple_of(dst * chunks, chunks), chunks)], sem).start()
        return c

    lax.fori_loop(0, toks, issue, 0, unroll=DMA_UNROLL)

    for k in range(TOP_K):
        pltpu.make_async_copy(xn_ref, xs_ref.at[pl.ds(0, toks * chunks)], sem).wait()


def moe_dispatch(dest, xn_lin, xs_init, toks, chunks):
    n = xn_lin.shape[0] // chunks
    return pl.pallas_call(
        functools.partial(_dispatch_kernel, toks=toks, chunks=chunks),
        out_shape=jax.ShapeDtypeStruct(xs_init.shape, F32),
        grid=(n // toks,),
        in_specs=[pl.BlockSpec((1, 1, TOP_K * toks), lambda i: (i, 0, 0), memory_space=pltpu.SMEM),
                  pl.BlockSpec((toks * chunks, LANES), lambda i: (i, 0)),
                  pl.BlockSpec(memory_space=pl.ANY)],
        out_specs=pl.BlockSpec(memory_space=pl.ANY),
        scratch_shapes=[pltpu.SemaphoreType.DMA(())],
        input_output_aliases={2: 0},
        compiler_params=_cparams(("arbitrary",)),
        name="moe_dispatch",
    )(dest, xn_lin, xs_init)


def _moe_ffn_kernel(be_ref, nu_ref, first_ref, nxt_ref, par_ref, xs_ref, wg_hbm, wu_hbm, wd_hbm, ys_ref,
                    wgs_ref, wus_ref, wds_ref, wgb_ref, wub_ref, wdb_ref, sem, *, layer, bm, chunks):
    i = pl.program_id(0)

    def weight_copies(e, s):
        return (pltpu.make_async_copy(wg_hbm.at[layer, e], wgs_ref.at[s], sem.at[s, 0]),
                pltpu.make_async_copy(wu_hbm.at[layer, e], wus_ref.at[s], sem.at[s, 1]),
                pltpu.make_async_copy(wd_hbm.at[layer, e], wds_ref.at[s], sem.at[s, 2]))

    @pl.when(i == 0)
    def _():
        for cp in weight_copies(be_ref[0], 0):
            cp.start()

    for s in range(2):
        @pl.when((first_ref[i] == 1) & (par_ref[i] == s))
        def _():
            for cp in weight_copies(be_ref[i], s):
                cp.wait()

            @pl.when(nxt_ref[i] >= 0)
            def _():
                for cp in weight_copies(nxt_ref[i], 1 - s):
                    cp.start()

            wgb_ref[...] = wgs_ref[s].astype(BF16)
            wub_ref[...] = wus_ref[s].astype(BF16)
            wdb_ref[...] = wds_ref[s].astype(BF16)

    @pl.when(i < nu_ref[0])
    def _():
        x = jnp.concatenate([xs_ref[pl.ds(c, bm, stride=chunks), :] for c in range(chunks)], axis=-1).astype(BF16)
        gt = jnp.dot(x, wgb_ref[...], preferred_element_type=F32)
        up = jnp.dot(x, wub_ref[...], preferred_element_type=F32)
        mid = (jax.nn.silu(gt) * up).astype(BF16)
        y = jnp.dot(mid, wdb_ref[...], preferred_element_type=F32)
        for c in range(chunks):
            ys_ref[pl.ds(c, bm, stride=chunks), :] = y[:, c * LANES:(c + 1) * LANES]

    @pl.when(i >= nu_ref[0])
    def _():
        ys_ref[...] = jnp.zeros_like(ys_ref)


def moe_ffn(blk_expert, n_used, xs_lin, w_gate, w_up, w_down, layer, bm, chunks):
    n_blocks = xs_lin.shape[0] // (bm * chunks)
    _, _, d, de = w_gate.shape

    def row_map(i, be, nu, *_):
        return (jnp.minimum(i, nu[0] - 1), 0)

    ar = jnp.arange(n_blocks, dtype=I32)
    first = jnp.concatenate([jnp.ones((1,), I32), (blk_expert[1:] != blk_expert[:-1]).astype(I32)])
    parity = (jnp.cumsum(first) - 1) % 2
    start_pos = jnp.where(first == 1, ar, n_blocks)
    next_start = jnp.concatenate([lax.cummin(start_pos, reverse=True)[1:], jnp.full((1,), n_blocks, I32)])
    nxt = jnp.where(next_start < n_blocks, blk_expert[jnp.minimum(next_start, n_blocks - 1)], -1).astype(I32)

    any_spec = pl.BlockSpec(memory_space=pl.ANY)
    return pl.pallas_call(
        functools.partial(_moe_ffn_kernel, layer=layer, bm=bm, chunks=chunks),
        out_shape=jax.ShapeDtypeStruct(xs_lin.shape, F32),
        grid_spec=pltpu.PrefetchScalarGridSpec(
            num_scalar_prefetch=5,
            grid=(n_blocks,),
            in_specs=[pl.BlockSpec((bm * chunks, LANES), row_map), any_spec, any_spec, any_spec],
            out_specs=pl.BlockSpec((bm * chunks, LANES), lambda i, *_: (i, 0)),
            scratch_shapes=[pltpu.VMEM((2, d, de), F32), pltpu.VMEM((2, d, de), F32), pltpu.VMEM((2, de, d), F32),
                            pltpu.VMEM((d, de), BF16), pltpu.VMEM((d, de), BF16), pltpu.VMEM((de, d), BF16),
                            pltpu.SemaphoreType.DMA((2, 3))]),
        compiler_params=_cparams(("arbitrary",)),
        name="moe_ffn",
    )(blk_expert, n_used, first, nxt, parity.astype(I32), xs_lin, w_gate, w_up, w_down)


def _slab_pitch(chunks):
    return chunks + 2


def _combine_kernel(dc_ref, dn_ref, ys_ref, h_ref, mf_ref, g_ref, *rest, toks, chunks, n_tiles, nb):
    if nb:
        fo_ref, buf_ref, ub_ref, sem = rest
    else:
        ho_ref, uo_ref, buf_ref, sem = rest
    i = pl.program_id(0)
    slab = _slab_pitch(chunks)
    slot_rows = toks * TOP_K * slab
    stride = TOP_K * slab

    def start_tile(dest_ref, slot):
        def body(r, c):
            for k in range(TOP_K):
                src = dest_ref[0, 0, TOP_K * r + k]
                pltpu.make_async_copy(ys_ref.at[pl.ds(pl.multiple_of(src * chunks, chunks), chunks)],
                                      buf_ref.at[pl.ds(slot * slot_rows + (r * TOP_K + k) * slab, chunks)],
                                      sem.at[slot]).start()
            return c

        lax.fori_loop(0, toks, body, 0, unroll=DMA_UNROLL)

    def wait_tile(slot):
        rows = toks * TOP_K * chunks
        pltpu.make_async_copy(ys_ref.at[pl.ds(0, rows)], buf_ref.at[pl.ds(slot * slot_rows, rows)],
                              sem.at[slot]).wait()

    def run(slot):
        if slot == 0:
            @pl.when(i == 0)
            def _():
                start_tile(dc_ref, 0)

        @pl.when(i + 1 < n_tiles)
        def _():
            start_tile(dn_ref, 1 - slot)

        wait_tile(slot)
        base = slot * slot_rows
        y0 = jnp.concatenate([buf_ref[pl.ds(base + c, toks, stride=stride), :] for c in range(chunks)], axis=-1)
        y1 = jnp.concatenate([buf_ref[pl.ds(base + slab + c, toks, stride=stride), :] for c in range(chunks)],
                             axis=-1)
        mf = mf_ref[...]
        h = h_ref[...] + mf[:, 0:1] * y0 + mf[:, 1:2] * y1
        if nb:
            u = _rms(h, g_ref[...])
            for c in range(chunks):
                ub_ref[c] = u[:, c * LANES:(c + 1) * LANES]
            for b in range(nb):
                fo_ref[b] = jnp.concatenate(
                    [ub_ref[c, pl.ds(b, toks // nb, stride=nb), :] for c in range(chunks)], axis=-1)
        else:
            ho_ref[...] = h
            uo_ref[...] = _rms(h, g_ref[...]).astype(uo_ref.dtype)

    @pl.when(i % 2 == 0)
    def _():
        run(0)

    @pl.when(i % 2 == 1)
    def _():
        run(1)


def moe_combine(dest, ys_lin, h, mf, g_next, u_dtype, toks, chunks, nb=0):
    n, d = h.shape
    n_tiles = n // toks
    row_spec = pl.BlockSpec((toks, d), lambda i: (i, 0))
    gather_buf = pltpu.VMEM((2 * toks * TOP_K * _slab_pitch(chunks), LANES), F32)
    if nb:
        out_shape = jax.ShapeDtypeStruct((nb, n // nb, d), F32)
        out_specs = pl.BlockSpec((nb, toks // nb, d), lambda i: (0, i, 0))
        scratch = [gather_buf, pltpu.VMEM((chunks, toks, LANES), F32), pltpu.SemaphoreType.DMA((2,))]
    else:
        out_shape = (jax.ShapeDtypeStruct((n, d), F32), jax.ShapeDtypeStruct((n, d), u_dtype))
        out_specs = (row_spec, row_spec)
        scratch = [gather_buf, pltpu.SemaphoreType.DMA((2,))]
    return pl.pallas_call(
        functools.partial(_combine_kernel, toks=toks, chunks=chunks, n_tiles=n_tiles, nb=nb),
        out_shape=out_shape,
        grid=(n_tiles,),
        in_specs=[pl.BlockSpec((1, 1, TOP_K * toks), lambda i: (i, 0, 0), memory_space=pltpu.SMEM),
                  pl.BlockSpec((1, 1, TOP_K * toks), lambda i: (jnp.minimum(i + 1, n_tiles - 1), 0, 0),
                               memory_space=pltpu.SMEM),
                  pl.BlockSpec(memory_space=pl.ANY),
                  row_spec,
                  pl.BlockSpec((toks, LANES), lambda i: (i, 0)),
                  pl.BlockSpec((1, d), lambda i: (0, 0))],
        out_specs=out_specs,
        scratch_shapes=scratch,
        compiler_params=_cparams(("arbitrary",)),
        name="moe_combine",
    )(dest, dest, ys_lin, h, mf, g_next.reshape(1, d))


def hier_moe_layer(h, g_ffn, w_group, w_expert, w_gate, w_up, w_down, layer, g_next, u_dtype, row_buf,
                   *, rows, bm, toks_d, toks_c, final_nb=0):
    n, d = h.shape
    chunks = d // LANES
    n_exp = w_gate.shape[1]
    xn_lin, mf, mi, cnt = router(h, g_ffn, w_group, w_expert, rows)

    counts = cnt[0, :n_exp].astype(I32)
    padded = (counts + bm - 1) // bm * bm
    pad_end = jnp.cumsum(padded)
    pad_start = pad_end - padded
    n_rows = n * TOP_K + n_exp * bm
    n_blocks = n_rows // bm
    blk_start = jnp.arange(n_blocks, dtype=I32) * bm
    blk_expert = jnp.sum((blk_start[:, None] >= pad_end[None, :]).astype(I32), axis=1)
    blk_expert = jnp.minimum(blk_expert, n_exp - 1).astype(I32)
    n_used = (pad_end[-1:] // bm).astype(I32)
    last_e = blk_expert[jnp.maximum(n_used[0] - 1, 0)]
    blk_expert = jnp.where(jnp.arange(n_blocks) < n_used[0], blk_expert, last_e)

    eid = mi[:, 0:TOP_K]
    seg = jnp.sum(jnp.where(eid[:, :, None] == jnp.arange(n_exp, dtype=I32), pad_start, 0), axis=-1)
    dest = (seg + mi[:, TOP_K:2 * TOP_K]).astype(I32)
    if row_buf is None:
        row_buf = jnp.zeros((n_rows * chunks, LANES), F32)
    xs_lin = moe_dispatch(dest.reshape(n // toks_d, 1, TOP_K * toks_d), xn_lin, row_buf, toks_d, chunks)
    ys_lin = moe_ffn(blk_expert, n_used, xs_lin, w_gate, w_up, w_down, layer, bm, chunks)
    dest_c = dest.reshape(n // toks_c, 1, TOP_K * toks_c)
    if final_nb:
        return None, moe_combine(dest_c, ys_lin, h, mf, g_next, u_dtype, toks_c, chunks, final_nb), None
    h_out, u_out = moe_combine(dest_c, ys_lin, h, mf, g_next, u_dtype, toks_c, chunks)
    return h_out, u_out, ys_lin


def trunk(x, norm_mix, norm_ffn, norm_final, ssm_a_re, ssm_a_im, ssm_log_dt, ssm_b_re, ssm_b_im,
          ssm_c_re, ssm_c_im, ssm_d, ssm_w_glu_a, ssm_w_glu_b, conv_w_in, conv_w, conv_w_out,
          moe_w_group_router, moe_w_expert_router, moe_w_gate, moe_w_up, moe_w_down,
          *, rows, steps, cols, cols_out, bm, toks_d, toks_c, cblock, cparts):
    bsz, seq, d = x.shape
    depth = norm_mix.shape[0]
    hh = ssm_b_re.shape[3]
    gb = cblock // hh

    h, u = norm_in(x, norm_mix[0], steps)
    row_buf = None
    for i in range(depth):
        j = i // 2
        if i % 2 == 0:
            bb_re, bb_im, l_re, l_im = ssm_prep(ssm_a_re[j], ssm_a_im[j], ssm_log_dt[j], ssm_b_re[j], ssm_b_im[j])
            wb, wc, lre, lim = _ssm_block_weights(bb_re, bb_im, ssm_c_re[j], ssm_c_im[j], l_re, l_im, gb)
            yg = s5_scan(u, wb, lre, lim, wc, ssm_d[j].astype(F32), bsz, steps, cparts)
            h = glu_residual(yg, ssm_w_glu_a[j].astype(BF16), ssm_w_glu_b[j].astype(BF16), h, rows, cols_out)
        else:
            gy = conv_in(u, conv_w_in[j].astype(BF16), conv_w[j].astype(F32), bsz, rows, cols)
            h = matmul_residual(gy, conv_w_out[j].astype(BF16), h, rows, cols_out)
        last = i == depth - 1
        g_next = norm_final if last else norm_mix[i + 1]
        u_dtype = F32 if (last or (i + 1) % 2 == 0) else BF16
        h, u, row_buf = hier_moe_layer(h, norm_ffn[i], moe_w_group_router[i], moe_w_expert_router[i],
                                       moe_w_gate, moe_w_up, moe_w_down, i, g_next, u_dtype, row_buf,
                                       rows=rows, bm=bm, toks_d=toks_d, toks_c=toks_c,
                                       final_nb=bsz if last else 0)
    return u


def kernel(x, norm_mix, norm_ffn, norm_final, ssm_a_re, ssm_a_im, ssm_log_dt, ssm_b_re, ssm_b_im, ssm_c_re, ssm_c_im, ssm_d, ssm_w_glu_a, ssm_w_glu_b, conv_w_in, conv_w, conv_w_out, moe_w_group_router, moe_w_expert_router, moe_w_gate, moe_w_up, moe_w_down):
    return trunk(x, norm_mix, norm_ffn, norm_final, ssm_a_re, ssm_a_im, ssm_log_dt, ssm_b_re, ssm_b_im,
                 ssm_c_re, ssm_c_im, ssm_d, ssm_w_glu_a, ssm_w_glu_b, conv_w_in, conv_w, conv_w_out,
                 moe_w_group_router, moe_w_expert_router, moe_w_gate, moe_w_up, moe_w_down,
                 rows=512, steps=64, cols=512, cols_out=1024, bm=256, toks_d=512, toks_c=128, cblock=256, cparts=4)
```

```python
import functools

import jax
import jax.numpy as jnp
from jax import lax
from jax.experimental import pallas as pl
from jax.experimental.pallas import tpu as pltpu

F32 = jnp.float32
BF16 = jnp.bfloat16
I32 = jnp.int32
U32 = jnp.uint32

LANES = 128
SUBLANES = 8
VMEM_LIMIT = 56 * 1024 * 1024
RMS_EPS = 1e-6
TOP_K = 2
DMA_UNROLL = 8


def _cparams(sem):
    return pltpu.CompilerParams(dimension_semantics=sem, vmem_limit_bytes=VMEM_LIMIT)


def _rms(h, g):
    ms = jnp.mean(h * h, axis=-1, keepdims=True)
    return h * lax.rsqrt(ms + RMS_EPS) * g


def _norm_in_kernel(x_ref, g_ref, h_ref, u_ref, t_ref, *, nb, steps):
    chunks = t_ref.shape[0]
    for b in range(nb):
        xb = x_ref[b]
        for c in range(chunks):
            t_ref[c, pl.ds(b, steps, stride=nb), :] = xb[:, c * LANES:(c + 1) * LANES]
    h = jnp.concatenate([t_ref[c] for c in range(chunks)], axis=-1)
    h_ref[...] = h
    u_ref[...] = _rms(h, g_ref[...])


def norm_in(x, g, steps):
    nb, seq, d = x.shape
    rows = steps * nb
    row_spec = pl.BlockSpec((rows, d), lambda i: (i, 0))
    return pl.pallas_call(
        functools.partial(_norm_in_kernel, nb=nb, steps=steps),
        out_shape=(jax.ShapeDtypeStruct((nb * seq, d), F32), jax.ShapeDtypeStruct((nb * seq, d), F32)),
        grid=(seq // steps,),
        in_specs=[pl.BlockSpec((nb, steps, d), lambda i: (0, i, 0)),
                  pl.BlockSpec((1, d), lambda i: (0, 0))],
        out_specs=(row_spec, row_spec),
        scratch_shapes=[pltpu.VMEM((d // LANES, rows, LANES), F32)],
        compiler_params=_cparams(("parallel",)),
        name="norm_in",
    )(x, g.reshape(1, d))


def _zoh(a_re, a_im, log_dt):
    dt = jnp.exp(log_dt)
    mag = jnp.exp(a_re * dt)
    return mag * jnp.cos(a_im * dt), mag * jnp.sin(a_im * dt)


def _ssm_prep_kernel(are_ref, aim_ref, ldt_ref, bre_ref, bim_ref,
                     are2_ref, aim2_ref, ldt2_ref,
                     bbre_ref, bbim_ref, lre_ref, lim_ref):
    a_re = are_ref[...]
    a_im = aim_ref[...]
    lb_re, lb_im = _zoh(a_re, a_im, ldt_ref[...])
    den = a_re * a_re + a_im * a_im
    nr = lb_re - 1.0
    ni = lb_im
    coef_re = (nr * a_re + ni * a_im) / den
    coef_im = (ni * a_re - nr * a_im) / den
    br = bre_ref[...]
    bi = bim_ref[...]
    bbre_ref[...] = coef_re * br - coef_im * bi
    bbim_ref[...] = coef_re * bi + coef_im * br
    l_re, l_im = _zoh(are2_ref[...], aim2_ref[...], ldt2_ref[...])
    lre_ref[...] = l_re
    lim_ref[...] = l_im


def ssm_prep(a_re, a_im, log_dt, b_re, b_im):
    g, p, hh = b_re.shape
    rep = lambda a: jnp.repeat(a.astype(F32), hh, axis=1)
    ldt2 = jnp.broadcast_to(log_dt.astype(F32)[:, None], (g, p))
    outs = pl.pallas_call(
        _ssm_prep_kernel,
        out_shape=(jax.ShapeDtypeStruct((g, p * hh), F32), jax.ShapeDtypeStruct((g, p * hh), F32),
                   jax.ShapeDtypeStruct((g, p), F32), jax.ShapeDtypeStruct((g, p), F32)),
        name="ssm_prep",
    )(rep(a_re), rep(a_im), rep(ldt2), b_re.astype(F32).reshape(g, p * hh), b_im.astype(F32).reshape(g, p * hh),
      a_re.astype(F32), a_im.astype(F32), ldt2)
    bb_re, bb_im, l_re, l_im = outs
    return bb_re.reshape(g, p, hh), bb_im.reshape(g, p, hh), l_re, l_im


def _ssm_block_weights(bb_re, bb_im, c_re, c_im, l_re, l_im, gb):
    g, p, hh = bb_re.shape
    j = g // gb
    eye = jnp.eye(gb, dtype=F32)

    def emb_b(bb):
        return jnp.einsum('jgph,gk->jghkp', bb.reshape(j, gb, p, hh), eye).reshape(j, gb * hh, gb * p)

    def emb_c(c):
        return jnp.einsum('jghp,gk->jkpgh', c.reshape(j, gb, hh, p), eye).reshape(j, gb * p, gb * hh)

    wb = jnp.concatenate([emb_b(bb_re), emb_b(bb_im)], axis=2).astype(BF16)
    wc = jnp.concatenate([emb_c(c_re.astype(F32)), -emb_c(c_im.astype(F32))], axis=1).astype(BF16)
    lre = l_re.reshape(j, 1, gb * p)
    lim = l_im.reshape(j, 1, gb * p)
    return wb, wc, lre, lim


def _s5_kernel(u_ref, wb_ref, lre_ref, lim_ref, wc_ref, d_ref, o_ref,
               bu_ref, sre_ref, sim_ref, *, steps, nb, ns, cb, parts):
    @pl.when(pl.program_id(1) == 0)
    def _():
        sre_ref[...] = jnp.zeros_like(sre_ref)
        sim_ref[...] = jnp.zeros_like(sim_ref)

    def project_in(p):
        u = u_ref[:, p * cb:(p + 1) * cb]
        bu_ref[p] = jnp.dot(u.astype(BF16), wb_ref[p], preferred_element_type=F32)

    project_in(0)
    for p in range(parts):
        if p + 1 < parts:
            project_in(p + 1)
        lr = jnp.broadcast_to(lre_ref[p], (nb, ns))
        li = jnp.broadcast_to(lim_ref[p], (nb, ns))
        sr = sre_ref[p]
        si = sim_ref[p]
        for t in range(steps):
            br = bu_ref[p, t * nb:(t + 1) * nb, 0:ns]
            bi = bu_ref[p, t * nb:(t + 1) * nb, ns:2 * ns]
            sr, si = lr * sr - li * si + br, lr * si + li * sr + bi
            bu_ref[p, t * nb:(t + 1) * nb, 0:ns] = sr
            bu_ref[p, t * nb:(t + 1) * nb, ns:2 * ns] = si
        sre_ref[p] = sr
        sim_ref[p] = si
        y = jnp.dot(bu_ref[p].astype(BF16), wc_ref[p], preferred_element_type=F32)
        y = y + d_ref[:, p * cb:(p + 1) * cb] * u_ref[:, p * cb:(p + 1) * cb]
        o_ref[:, p * cb:(p + 1) * cb] = jax.nn.gelu(y).astype(o_ref.dtype)


def s5_scan(u, wb, lre, lim, wc, d_skip, nb, steps, parts):
    n, d = u.shape
    j, cb, ns2 = wb.shape
    ns = ns2 // 2
    rows = steps * nb
    jp = j // parts
    wb = wb.reshape(jp, parts, cb, ns2)
    wc = wc.reshape(jp, parts, ns2, cb)
    lre = lre.reshape(jp, parts, 1, ns)
    lim = lim.reshape(jp, parts, 1, ns)
    return pl.pallas_call(
        functools.partial(_s5_kernel, steps=steps, nb=nb, ns=ns, cb=cb, parts=parts),
        out_shape=jax.ShapeDtypeStruct((n, d), BF16),
        grid=(jp, n // rows),
        in_specs=[pl.BlockSpec((rows, parts * cb), lambda jj, i: (i, jj)),
                  pl.BlockSpec((None, parts, cb, ns2), lambda jj, i: (jj, 0, 0, 0)),
                  pl.BlockSpec((None, parts, 1, ns), lambda jj, i: (jj, 0, 0, 0)),
                  pl.BlockSpec((None, parts, 1, ns), lambda jj, i: (jj, 0, 0, 0)),
                  pl.BlockSpec((None, parts, ns2, cb), lambda jj, i: (jj, 0, 0, 0)),
                  pl.BlockSpec((1, parts * cb), lambda jj, i: (0, jj))],
        out_specs=pl.BlockSpec((rows, parts * cb), lambda jj, i: (i, jj)),
        scratch_shapes=[pltpu.VMEM((parts, rows, ns2), F32),
                        pltpu.VMEM((parts, nb, ns), F32),
                        pltpu.VMEM((parts, nb, ns), F32)],
        compiler_params=_cparams(("parallel", "arbitrary")),
        name="s5_scan",
    )(u, wb, lre, lim, wc, d_skip.reshape(1, d))


def _glu_kernel(y_ref, wa_ref, wb_ref, h_ref, o_ref):
    y = y_ref[...]
    a = jnp.dot(y, wa_ref[...], preferred_element_type=F32)
    b = jnp.dot(y, wb_ref[...], preferred_element_type=F32)
    o_ref[...] = h_ref[...] + a * jax.nn.sigmoid(b)


def glu_residual(y, wa, wb, h, rows, cols):
    n, d = y.shape
    dn = wa.shape[1]
    return pl.pallas_call(
        _glu_kernel,
        out_shape=jax.ShapeDtypeStruct((n, dn), F32),
        grid=(dn // cols, n // rows),
        in_specs=[pl.BlockSpec((rows, d), lambda c, i: (i, 0)),
                  pl.BlockSpec((d, cols), lambda c, i: (0, c)),
                  pl.BlockSpec((d, cols), lambda c, i: (0, c)),
                  pl.BlockSpec((rows, cols), lambda c, i: (i, c))],
        out_specs=pl.BlockSpec((rows, cols), lambda c, i: (i, c)),
        compiler_params=_cparams(("parallel", "parallel")),
        name="glu_residual",
    )(y, wa, wb, h)


def _mm_res_kernel(x_ref, w_ref, h_ref, o_ref):
    o_ref[...] = h_ref[...] + jnp.dot(x_ref[...], w_ref[...], preferred_element_type=F32)


def matmul_residual(x, w, h, rows, cols):
    n, d = x.shape
    dn = w.shape[1]
    return pl.pallas_call(
        _mm_res_kernel,
        out_shape=jax.ShapeDtypeStruct((n, dn), F32),
        grid=(dn // cols, n // rows),
        in_specs=[pl.BlockSpec((rows, d), lambda c, i: (i, 0)),
                  pl.BlockSpec((d, cols), lambda c, i: (0, c)),
                  pl.BlockSpec((rows, cols), lambda c, i: (i, c))],
        out_specs=pl.BlockSpec((rows, cols), lambda c, i: (i, c)),
        compiler_params=_cparams(("parallel", "parallel")),
        name="matmul_residual",
    )(x, w, h)


def _conv_in_kernel(u_ref, wbg_ref, wcg_ref, wv_ref, cw_ref, o_ref, z_ref, *, rows, nb, width):
    halo = (width - 1) * nb

    @pl.when(pl.program_id(1) == 0)
    def _():
        z_ref[0:halo, :] = jnp.zeros((halo, z_ref.shape[1]), F32)

    u = u_ref[...]
    bg = jnp.dot(u, wbg_ref[...], preferred_element_type=F32)
    cg = jnp.dot(u, wcg_ref[...], preferred_element_type=F32)
    v = jnp.dot(u, wv_ref[...], preferred_element_type=F32)
    z_ref[halo:halo + rows, :] = cg * v
    y = cw_ref[0:1, :] * z_ref[0:rows, :]
    for k in range(1, width):
        y = y + cw_ref[k:k + 1, :] * z_ref[k * nb:k * nb + rows, :]
    o_ref[...] = (bg * y).astype(o_ref.dtype)
    z_ref[0:halo, :] = z_ref[rows:rows + halo, :]


def conv_in(u, w_in, conv_w, nb, rows, cols):
    n, d = u.shape
    width = conv_w.shape[0]
    nc = d // cols
    halo = (width - 1) * nb
    return pl.pallas_call(
        functools.partial(_conv_in_kernel, rows=rows, nb=nb, width=width),
        out_shape=jax.ShapeDtypeStruct((n, d), BF16),
        grid=(nc, n // rows),
        in_specs=[pl.BlockSpec((rows, d), lambda c, i: (i, 0)),
                  pl.BlockSpec((d, cols), lambda c, i: (0, c)),
                  pl.BlockSpec((d, cols), lambda c, i: (0, c + nc)),
                  pl.BlockSpec((d, cols), lambda c, i: (0, c + 2 * nc)),
                  pl.BlockSpec((width, cols), lambda c, i: (0, c))],
        out_specs=pl.BlockSpec((rows, cols), lambda c, i: (i, c)),
        scratch_shapes=[pltpu.VMEM((rows + halo, cols), F32)],
        compiler_params=_cparams(("parallel", "arbitrary")),
        name="conv_in",
    )(u, w_in, w_in, w_in, conv_w)


def _router_kernel(h_ref, g_ref, whi_ref, wlo_ref, tri_ref, xn_ref, mf_ref, mi_ref, cnt_ref,
                   *, rows, d, n_groups, epg):
    @pl.when(pl.program_id(0) == 0)
    def _():
        cnt_ref[...] = jnp.zeros_like(cnt_ref)

    xn = _rms(h_ref[...], g_ref[...])
    x_hi = xn.astype(BF16)
    x_hi32 = x_hi.astype(F32)

    bits = lax.bitcast_convert_type(x_hi32, U32)
    words = d // LANES // 2
    for k in range(words):
        hi = bits[:, (2 * k) * LANES:(2 * k + 1) * LANES]
        lo = bits[:, (2 * k + 1) * LANES:(2 * k + 2) * LANES]
        xn_ref[pl.ds(k, rows, stride=words), :] = hi | (lo >> 16)

    x_lo = (xn - x_hi32).astype(BF16)
    w_hi = whi_ref[...]
    logits = (jnp.dot(x_hi, w_hi, preferred_element_type=F32)
              + jnp.dot(x_lo, w_hi, preferred_element_type=F32)
              + jnp.dot(x_hi, wlo_ref[...], preferred_element_type=F32))

    lane = lax.broadcasted_iota(I32, (rows, LANES), 1).astype(F32)
    neg = jnp.float32(-jnp.inf)
    big = jnp.float32(LANES)

    def first_max(vals, mask):
        m = jnp.max(jnp.where(mask, vals, neg), axis=-1, keepdims=True)
        idx = jnp.min(jnp.where(mask & (vals == m), lane, big), axis=-1, keepdims=True)
        return m, idx

    gmask = lane < n_groups
    gmax, gidx = first_max(logits, gmask)
    gsum = jnp.sum(jnp.where(gmask, jnp.exp(logits - gmax), 0.0), axis=-1, keepdims=True)
    g_w = 1.0 / gsum
    lo = n_groups + gidx * epg
    emask = (lane >= lo) & (lane < lo + epg)
    t1, i1 = first_max(logits, emask)
    t2, i2 = first_max(logits, emask & (lane != i1))
    e2 = jnp.exp(t2 - t1)
    p1 = 1.0 / (1.0 + e2)
    p2 = e2 / (1.0 + e2)
    eid1 = i1 - n_groups
    eid2 = i2 - n_groups

    onehot = ((lane == eid1) | (lane == eid2))
    oh = jnp.where(onehot, 1.0, 0.0).astype(BF16)
    before = jnp.dot(tri_ref[...], oh, preferred_element_type=F32) + cnt_ref[...]
    r1 = jnp.sum(jnp.where(lane == eid1, before, 0.0), axis=-1, keepdims=True)
    r2 = jnp.sum(jnp.where(lane == eid2, before, 0.0), axis=-1, keepdims=True)
    cnt_ref[...] = cnt_ref[...] + jnp.sum(oh.astype(F32), axis=0, keepdims=True)

    mf_ref[...] = jnp.where(lane == 0, g_w * p1, jnp.where(lane == 1, g_w * p2, 0.0))
    mi_ref[...] = jnp.where(lane == 0, eid1, jnp.where(lane == 1, eid2,
                            jnp.where(lane == 2, r1, jnp.where(lane == 3, r2, 0.0)))).astype(I32)


def router(h, g, w_group, w_expert, rows):
    n, d = h.shape
    n_groups = w_group.shape[1]
    epg = w_expert.shape[2]
    n_exp = n_groups * epg
    assert n_groups + n_exp <= LANES
    words = d // LANES // 2
    w = jnp.concatenate([w_group.astype(F32),
                         jnp.transpose(w_expert.astype(F32), (1, 0, 2)).reshape(d, n_exp)], axis=1)
    w = jnp.pad(w, ((0, 0), (0, LANES - n_groups - n_exp)))
    w_hi = w.astype(BF16)
    w_lo = (w - w_hi.astype(F32)).astype(BF16)
    tri = (lax.broadcasted_iota(I32, (rows, rows), 1) < lax.broadcasted_iota(I32, (rows, rows), 0)).astype(BF16)
    xn, mf, mi, cnt = pl.pallas_call(
        functools.partial(_router_kernel, rows=rows, d=d, n_groups=n_groups, epg=epg),
        out_shape=(jax.ShapeDtypeStruct((n * words, LANES), U32),
                   jax.ShapeDtypeStruct((n, LANES), F32),
                   jax.ShapeDtypeStruct((n, LANES), I32),
                   jax.ShapeDtypeStruct((1, LANES), F32)),
        grid=(n // rows,),
        in_specs=[pl.BlockSpec((rows, d), lambda i: (i, 0)),
                  pl.BlockSpec((1, d), lambda i: (0, 0)),
                  pl.BlockSpec((d, LANES), lambda i: (0, 0)),
                  pl.BlockSpec((d, LANES), lambda i: (0, 0)),
                  pl.BlockSpec((rows, rows), lambda i: (0, 0))],
        out_specs=(pl.BlockSpec((rows * words, LANES), lambda i: (i, 0)),
                   pl.BlockSpec((rows, LANES), lambda i: (i, 0)),
                   pl.BlockSpec((rows, LANES), lambda i: (i, 0)),
                   pl.BlockSpec((1, LANES), lambda i: (0, 0))),
        compiler_params=_cparams(("arbitrary",)),
        name="moe_router",
    )(h, g.reshape(1, d), w_hi, w_lo, tri)
    return xn, mf, mi, cnt


def _dispatch_kernel(dest_ref, xn_ref, xs_in_ref, xs_ref, sem, *, toks, chunks):
    del xs_in_ref

    def issue(r, c):
        src = xn_ref.at[pl.ds(pl.multiple_of(r * chunks, chunks), chunks)]
        for k in range(TOP_K):
            dst = dest_ref[0, 0, TOP_K * r + k]
            pltpu.make_async_copy(src, xs_ref.at[pl.ds(pl.multiple_of(dst * chunks, chunks), chunks)], sem).start()
        return c

    lax.fori_loop(0, toks, issue, 0, unroll=DMA_UNROLL)

    for k in range(TOP_K):
        pltpu.make_async_copy(xn_ref, xs_ref.at[pl.ds(0, toks * chunks)], sem).wait()


def moe_dispatch(dest, xn_lin, xs_init, toks, chunks):
    n = xn_lin.shape[0] // chunks
    return pl.pallas_call(
        functools.partial(_dispatch_kernel, toks=toks, chunks=chunks),
        out_shape=jax.ShapeDtypeStruct(xs_init.shape, xs_init.dtype),
        grid=(n // toks,),
        in_specs=[pl.BlockSpec((1, 1, TOP_K * toks), lambda i: (i, 0, 0), memory_space=pltpu.SMEM),
                  pl.BlockSpec((toks * chunks, LANES), lambda i: (i, 0)),
                  pl.BlockSpec(memory_space=pl.ANY)],
        out_specs=pl.BlockSpec(memory_space=pl.ANY),
        scratch_shapes=[pltpu.SemaphoreType.DMA(())],
        input_output_aliases={2: 0},
        compiler_params=_cparams(("arbitrary",)),
        name="moe_dispatch",
    )(dest, xn_lin, xs_init)


def _moe_ffn_kernel(be_ref, nu_ref, first_ref, nxt_ref, par_ref, xs_ref, wg_hbm, wu_hbm, wd_hbm, ys_ref,
                    wgs_ref, wus_ref, wds_ref, wgb_ref, wub_ref, wdb_ref, sem, *, layer, bm, chunks):
    i = pl.program_id(0)

    def weight_copies(e, s):
        return (pltpu.make_async_copy(wg_hbm.at[layer, e], wgs_ref.at[s], sem.at[s, 0]),
                pltpu.make_async_copy(wu_hbm.at[layer, e], wus_ref.at[s], sem.at[s, 1]),
                pltpu.make_async_copy(wd_hbm.at[layer, e], wds_ref.at[s], sem.at[s, 2]))

    @pl.when(i == 0)
    def _():
        for cp in weight_copies(be_ref[0], 0):
            cp.start()

    for s in range(2):
        @pl.when((first_ref[i] == 1) & (par_ref[i] == s))
        def _():
            for cp in weight_copies(be_ref[i], s):
                cp.wait()

            @pl.when(nxt_ref[i] >= 0)
            def _():
                for cp in weight_copies(nxt_ref[i], 1 - s):
                    cp.start()

            wgb_ref[...] = wgs_ref[s].astype(BF16)
            wub_ref[...] = wus_ref[s].astype(BF16)
            wdb_ref[...] = wds_ref[s].astype(BF16)

    @pl.when(i < nu_ref[0])
    def _():
        words = chunks // 2
        halves = []
        for k in range(words):
            w = xs_ref[pl.ds(k, bm, stride=words), :]
            halves.append(lax.bitcast_convert_type(w & jnp.uint32(0xFFFF0000), F32).astype(BF16))
            halves.append(lax.bitcast_convert_type(w << 16, F32).astype(BF16))
        x = jnp.concatenate(halves, axis=-1)
        gt = jnp.dot(x, wgb_ref[...], preferred_element_type=F32)
        up = jnp.dot(x, wub_ref[...], preferred_element_type=F32)
        mid = (jax.nn.silu(gt) * up).astype(BF16)
        y = jnp.dot(mid, wdb_ref[...], preferred_element_type=F32)
        for c in range(chunks):
            ys_ref[pl.ds(c, bm, stride=chunks), :] = y[:, c * LANES:(c + 1) * LANES]

    @pl.when(i >= nu_ref[0])
    def _():
        ys_ref[...] = jnp.zeros_like(ys_ref)


def moe_ffn(blk_expert, n_used, xs_lin, w_gate, w_up, w_down, layer, bm, chunks):
    words = chunks // 2
    n_blocks = xs_lin.shape[0] // (bm * words)
    _, _, d, de = w_gate.shape

    def row_map(i, be, nu, *_):
        return (jnp.minimum(i, nu[0] - 1), 0)

    ar = jnp.arange(n_blocks, dtype=I32)
    first = jnp.concatenate([jnp.ones((1,), I32), (blk_expert[1:] != blk_expert[:-1]).astype(I32)])
    parity = (jnp.cumsum(first) - 1) % 2
    start_pos = jnp.where(first == 1, ar, n_blocks)
    next_start = jnp.concatenate([lax.cummin(start_pos, reverse=True)[1:], jnp.full((1,), n_blocks, I32)])
    nxt = jnp.where(next_start < n_blocks, blk_expert[jnp.minimum(next_start, n_blocks - 1)], -1).astype(I32)

    any_spec = pl.BlockSpec(memory_space=pl.ANY)
    return pl.pallas_call(
        functools.partial(_moe_ffn_kernel, layer=layer, bm=bm, chunks=chunks),
        out_shape=jax.ShapeDtypeStruct((n_blocks * bm * chunks, LANES), F32),
        grid_spec=pltpu.PrefetchScalarGridSpec(
            num_scalar_prefetch=5,
            grid=(n_blocks,),
            in_specs=[pl.BlockSpec((bm * words, LANES), row_map), any_spec, any_spec, any_spec],
            out_specs=pl.BlockSpec((bm * chunks, LANES), lambda i, *_: (i, 0)),
            scratch_shapes=[pltpu.VMEM((2, d, de), F32), pltpu.VMEM((2, d, de), F32), pltpu.VMEM((2, de, d), F32),
                            pltpu.VMEM((d, de), BF16), pltpu.VMEM((d, de), BF16), pltpu.VMEM((de, d), BF16),
                            pltpu.SemaphoreType.DMA((2, 3))]),
        compiler_params=_cparams(("arbitrary",)),
        name="moe_ffn",
    )(blk_expert, n_used, first, nxt, parity.astype(I32), xs_lin, w_gate, w_up, w_down)


def _slab_pitch(chunks):
    return chunks + 2


def _combine_kernel(dc_ref, dn_ref, ys_ref, h_ref, mf_ref, g_ref, *rest, toks, chunks, n_tiles, nb):
    if nb:
        fo_ref, buf_ref, ub_ref, sem = rest
    else:
        ho_ref, uo_ref, buf_ref, sem = rest
    i = pl.program_id(0)
    slab = _slab_pitch(chunks)
    slot_rows = toks * TOP_K * slab
    stride = TOP_K * slab

    def start_tile(dest_ref, slot):
        def body(r, c):
            for k in range(TOP_K):
                src = dest_ref[0, 0, TOP_K * r + k]
                pltpu.make_async_copy(ys_ref.at[pl.ds(pl.multiple_of(src * chunks, chunks), chunks)],
                                      buf_ref.at[pl.ds(slot * slot_rows + (r * TOP_K + k) * slab, chunks)],
                                      sem.at[slot]).start()
            return c

        lax.fori_loop(0, toks, body, 0, unroll=DMA_UNROLL)

    def wait_tile(slot):
        rows = toks * TOP_K * chunks
        pltpu.make_async_copy(ys_ref.at[pl.ds(0, rows)], buf_ref.at[pl.ds(slot * slot_rows, rows)],
                              sem.at[slot]).wait()

    def run(slot):
        if slot == 0:
            @pl.when(i == 0)
            def _():
                start_tile(dc_ref, 0)

        @pl.when(i + 1 < n_tiles)
        def _():
            start_tile(dn_ref, 1 - slot)

        wait_tile(slot)
        base = slot * slot_rows
        y0 = jnp.concatenate([buf_ref[pl.ds(base + c, toks, stride=stride), :] for c in range(chunks)], axis=-1)
        y1 = jnp.concatenate([buf_ref[pl.ds(base + slab + c, toks, stride=stride), :] for c in range(chunks)],
                             axis=-1)
        mf = mf_ref[...]
        h = h_ref[...] + mf[:, 0:1] * y0 + mf[:, 1:2] * y1
        if nb:
            u = _rms(h, g_ref[...])
            for c in range(chunks):
                ub_ref[c] = u[:, c * LANES:(c + 1) * LANES]
            for b in range(nb):
                fo_ref[b] = jnp.concatenate(
                    [ub_ref[c, pl.ds(b, toks // nb, stride=nb), :] for c in range(chunks)], axis=-1)
        else:
            ho_ref[...] = h
            uo_ref[...] = _rms(h, g_ref[...]).astype(uo_ref.dtype)

    @pl.when(i % 2 == 0)
    def _():
        run(0)

    @pl.when(i % 2 == 1)
    def _():
        run(1)


def moe_combine(dest, ys_lin, h, mf, g_next, u_dtype, toks, chunks, nb=0):
    n, d = h.shape
    n_tiles = n // toks
    row_spec = pl.BlockSpec((toks, d), lambda i: (i, 0))
    gather_buf = pltpu.VMEM((2 * toks * TOP_K * _slab_pitch(chunks), LANES), F32)
    if nb:
        out_shape = jax.ShapeDtypeStruct((nb, n // nb, d), F32)
        out_specs = pl.BlockSpec((nb, toks // nb, d), lambda i: (0, i, 0))
        scratch = [gather_buf, pltpu.VMEM((chunks, toks, LANES), F32), pltpu.SemaphoreType.DMA((2,))]
    else:
        out_shape = (jax.ShapeDtypeStruct((n, d), F32), jax.ShapeDtypeStruct((n, d), u_dtype))
        out_specs = (row_spec, row_spec)
        scratch = [gather_buf, pltpu.SemaphoreType.DMA((2,))]
    return pl.pallas_call(
        functools.partial(_combine_kernel, toks=toks, chunks=chunks, n_tiles=n_tiles, nb=nb),
        out_shape=out_shape,
        grid=(n_tiles,),
        in_specs=[pl.BlockSpec((1, 1, TOP_K * toks), lambda i: (i, 0, 0), memory_space=pltpu.SMEM),
                  pl.BlockSpec((1, 1, TOP_K * toks), lambda i: (jnp.minimum(i + 1, n_tiles - 1), 0, 0),
                               memory_space=pltpu.SMEM),
                  pl.BlockSpec(memory_space=pl.ANY),
                  row_spec,
                  pl.BlockSpec((toks, LANES), lambda i: (i, 0)),
                  pl.BlockSpec((1, d), lambda i: (0, 0))],
        out_specs=out_specs,
        scratch_shapes=scratch,
        compiler_params=_cparams(("arbitrary",)),
        name="moe_combine",
    )(dest, dest, ys_lin, h, mf, g_next.reshape(1, d))


def hier_moe_layer(h, g_ffn, w_group, w_expert, w_gate, w_up, w_down, layer, g_next, u_dtype, row_buf,
                   *, rows, bm, toks_d, toks_c, final_nb=0):
    n, d = h.shape
    chunks = d // LANES
    n_exp = w_gate.shape[1]
    xn_lin, mf, mi, cnt = router(h, g_ffn, w_group, w_expert, rows)

    counts = cnt[0, :n_exp].astype(I32)
    padded = (counts + bm - 1) // bm * bm
    pad_end = jnp.cumsum(padded)
    pad_start = pad_end - padded
    n_rows = n * TOP_K + n_exp * bm
    n_blocks = n_rows // bm
    blk_start = jnp.arange(n_blocks, dtype=I32) * bm
    blk_expert = jnp.sum((blk_start[:, None] >= pad_end[None, :]).astype(I32), axis=1)
    blk_expert = jnp.minimum(blk_expert, n_exp - 1).astype(I32)
    n_used = (pad_end[-1:] // bm).astype(I32)
    last_e = blk_expert[jnp.maximum(n_used[0] - 1, 0)]
    blk_expert = jnp.where(jnp.arange(n_blocks) < n_used[0], blk_expert, last_e)

    eid = mi[:, 0:TOP_K]
    seg = jnp.sum(jnp.where(eid[:, :, None] == jnp.arange(n_exp, dtype=I32), pad_start, 0), axis=-1)
    dest = (seg + mi[:, TOP_K:2 * TOP_K]).astype(I32)
    words = chunks // 2
    if row_buf is None:
        row_buf = jnp.zeros((n_rows * words, LANES), U32)
    xs_lin = moe_dispatch(dest.reshape(n // toks_d, 1, TOP_K * toks_d), xn_lin, row_buf, toks_d, words)
    ys_lin = moe_ffn(blk_expert, n_used, xs_lin, w_gate, w_up, w_down, layer, bm, chunks)
    dest_c = dest.reshape(n // toks_c, 1, TOP_K * toks_c)
    if final_nb:
        return None, moe_combine(dest_c, ys_lin, h, mf, g_next, u_dtype, toks_c, chunks, final_nb), None
    h_out, u_out = moe_combine(dest_c, ys_lin, h, mf, g_next, u_dtype, toks_c, chunks)
    return h_out, u_out, xs_lin


def trunk(x, norm_mix, norm_ffn, norm_final, ssm_a_re, ssm_a_im, ssm_log_dt, ssm_b_re, ssm_b_im,
          ssm_c_re, ssm_c_im, ssm_d, ssm_w_glu_a, ssm_w_glu_b, conv_w_in, conv_w, conv_w_out,
          moe_w_group_router, moe_w_expert_router, moe_w_gate, moe_w_up, moe_w_down,
          *, rows, steps, cols, cols_out, bm, toks_d, toks_c, cblock, cparts):
    bsz, seq, d = x.shape
    depth = norm_mix.shape[0]
    hh = ssm_b_re.shape[3]
    gb = cblock // hh

    h, u = norm_in(x, norm_mix[0], steps)
    row_buf = None
    for i in range(depth):
        j = i // 2
        if i % 2 == 0:
            bb_re, bb_im, l_re, l_im = ssm_prep(ssm_a_re[j], ssm_a_im[j], ssm_log_dt[j], ssm_b_re[j], ssm_b_im[j])
            wb, wc, lre, lim = _ssm_block_weights(bb_re, bb_im, ssm_c_re[j], ssm_c_im[j], l_re, l_im, gb)
            yg = s5_scan(u, wb, lre, lim, wc, ssm_d[j].astype(F32), bsz, steps, cparts)
            h = glu_residual(yg, ssm_w_glu_a[j].astype(BF16), ssm_w_glu_b[j].astype(BF16), h, rows, cols_out)
        else:
            gy = conv_in(u, conv_w_in[j].astype(BF16), conv_w[j].astype(F32), bsz, rows, cols)
            h = matmul_residual(gy, conv_w_out[j].astype(BF16), h, rows, cols_out)
        last = i == depth - 1
        g_next = norm_final if last else norm_mix[i + 1]
        u_dtype = F32 if (last or (i + 1) % 2 == 0) else BF16
        h, u, row_buf = hier_moe_layer(h, norm_ffn[i], moe_w_group_router[i], moe_w_expert_router[i],
                                       moe_w_gate, moe_w_up, moe_w_down, i, g_next, u_dtype, row_buf,
                                       rows=rows, bm=bm, toks_d=toks_d, toks_c=toks_c,
                                       final_nb=bsz if last else 0)
    return u


def kernel(x, norm_mix, norm_ffn, norm_final, ssm_a_re, ssm_a_im, ssm_log_dt, ssm_b_re, ssm_b_im, ssm_c_re, ssm_c_im, ssm_d, ssm_w_glu_a, ssm_w_glu_b, conv_w_in, conv_w, conv_w_out, moe_w_group_router, moe_w_expert_router, moe_w_gate, moe_w_up, moe_w_down):
    return trunk(x, norm_mix, norm_ffn, norm_final, ssm_a_re, ssm_a_im, ssm_log_dt, ssm_b_re, ssm_b_im,
                 ssm_c_re, ssm_c_im, ssm_d, ssm_w_glu_a, ssm_w_glu_b, conv_w_in, conv_w, conv_w_out,
                 moe_w_group_router, moe_w_expert_router, moe_w_gate, moe_w_up, moe_w_down,
                 rows=512, steps=64, cols=512, cols_out=1024, bm=256, toks_d=512, toks_c=128, cblock=256, cparts=4)
```

```python
import functools

import jax
import jax.numpy as jnp
from jax import lax
from jax.experimental import pallas as pl
from jax.experimental.pallas import tpu as pltpu

F32 = jnp.float32
BF16 = jnp.bfloat16
I32 = jnp.int32
U32 = jnp.uint32

LANES = 128
SUBLANES = 8
VMEM_LIMIT = 56 * 1024 * 1024
RMS_EPS = 1e-6
TOP_K = 2
DMA_UNROLL = 8


def _cparams(sem):
    return pltpu.CompilerParams(dimension_semantics=sem, vmem_limit_bytes=VMEM_LIMIT)


def _rms(h, g):
    ms = jnp.mean(h * h, axis=-1, keepdims=True)
    return h * lax.rsqrt(ms + RMS_EPS) * g


def _norm_in_kernel(x_ref, g_ref, h_ref, u_ref, t_ref, *, nb, steps):
    chunks = t_ref.shape[0]
    for b in range(nb):
        xb = x_ref[b]
        for c in range(chunks):
            t_ref[c, pl.ds(b, steps, stride=nb), :] = xb[:, c * LANES:(c + 1) * LANES]
    h = jnp.concatenate([t_ref[c] for c in range(chunks)], axis=-1)
    h_ref[...] = h
    u_ref[...] = _rms(h, g_ref[...])


def norm_in(x, g, steps):
    nb, seq, d = x.shape
    rows = steps * nb
    row_spec = pl.BlockSpec((rows, d), lambda i: (i, 0))
    return pl.pallas_call(
        functools.partial(_norm_in_kernel, nb=nb, steps=steps),
        out_shape=(jax.ShapeDtypeStruct((nb * seq, d), F32), jax.ShapeDtypeStruct((nb * seq, d), F32)),
        grid=(seq // steps,),
        in_specs=[pl.BlockSpec((nb, steps, d), lambda i: (0, i, 0)),
                  pl.BlockSpec((1, d), lambda i: (0, 0))],
        out_specs=(row_spec, row_spec),
        scratch_shapes=[pltpu.VMEM((d // LANES, rows, LANES), F32)],
        compiler_params=_cparams(("parallel",)),
        name="norm_in",
    )(x, g.reshape(1, d))


def _zoh(a_re, a_im, log_dt):
    dt = jnp.exp(log_dt)
    mag = jnp.exp(a_re * dt)
    return mag * jnp.cos(a_im * dt), mag * jnp.sin(a_im * dt)


def _ssm_prep_kernel(are_ref, aim_ref, ldt_ref, bre_ref, bim_ref,
                     are2_ref, aim2_ref, ldt2_ref,
                     bbre_ref, bbim_ref, lre_ref, lim_ref):
    a_re = are_ref[...]
    a_im = aim_ref[...]
    lb_re, lb_im = _zoh(a_re, a_im, ldt_ref[...])
    den = a_re * a_re + a_im * a_im
    nr = lb_re - 1.0
    ni = lb_im
    coef_re = (nr * a_re + ni * a_im) / den
    coef_im = (ni * a_re - nr * a_im) / den
    br = bre_ref[...]
    bi = bim_ref[...]
    bbre_ref[...] = coef_re * br - coef_im * bi
    bbim_ref[...] = coef_re * bi + coef_im * br
    l_re, l_im = _zoh(are2_ref[...], aim2_ref[...], ldt2_ref[...])
    lre_ref[...] = l_re
    lim_ref[...] = l_im


def ssm_prep(a_re, a_im, log_dt, b_re, b_im):
    g, p, hh = b_re.shape
    rep = lambda a: jnp.repeat(a.astype(F32), hh, axis=1)
    ldt2 = jnp.broadcast_to(log_dt.astype(F32)[:, None], (g, p))
    outs = pl.pallas_call(
        _ssm_prep_kernel,
        out_shape=(jax.ShapeDtypeStruct((g, p * hh), F32), jax.ShapeDtypeStruct((g, p * hh), F32),
                   jax.ShapeDtypeStruct((g, p), F32), jax.ShapeDtypeStruct((g, p), F32)),
        name="ssm_prep",
    )(rep(a_re), rep(a_im), rep(ldt2), b_re.astype(F32).reshape(g, p * hh), b_im.astype(F32).reshape(g, p * hh),
      a_re.astype(F32), a_im.astype(F32), ldt2)
    bb_re, bb_im, l_re, l_im = outs
    return bb_re.reshape(g, p, hh), bb_im.reshape(g, p, hh), l_re, l_im


def _ssm_block_weights(bb_re, bb_im, c_re, c_im, l_re, l_im, gb):
    g, p, hh = bb_re.shape
    j = g // gb
    eye = jnp.eye(gb, dtype=F32)

    def emb_b(bb):
        return jnp.einsum('jgph,gk->jghkp', bb.reshape(j, gb, p, hh), eye).reshape(j, gb * hh, gb * p)

    def emb_c(c):
        return jnp.einsum('jghp,gk->jkpgh', c.reshape(j, gb, hh, p), eye).reshape(j, gb * p, gb * hh)

    wb = jnp.concatenate([emb_b(bb_re), emb_b(bb_im)], axis=2).astype(BF16)
    wc = jnp.concatenate([emb_c(c_re.astype(F32)), -emb_c(c_im.astype(F32))], axis=1).astype(BF16)
    lre = l_re.reshape(j, 1, gb * p)
    lim = l_im.reshape(j, 1, gb * p)
    return wb, wc, lre, lim


def _s5_kernel(u_ref, wb_ref, lre_ref, lim_ref, wc_ref, d_ref, o_ref,
               bu_ref, sre_ref, sim_ref, *, steps, nb, ns, cb, parts):
    @pl.when(pl.program_id(1) == 0)
    def _():
        sre_ref[...] = jnp.zeros_like(sre_ref)
        sim_ref[...] = jnp.zeros_like(sim_ref)

    def project_in(p):
        u = u_ref[:, p * cb:(p + 1) * cb]
        bu_ref[p] = jnp.dot(u.astype(BF16), wb_ref[p], preferred_element_type=F32)

    project_in(0)
    for p in range(parts):
        if p + 1 < parts:
            project_in(p + 1)
        lr = jnp.broadcast_to(lre_ref[p], (nb, ns))
        li = jnp.broadcast_to(lim_ref[p], (nb, ns))
        sr = sre_ref[p]
        si = sim_ref[p]
        for t in range(steps):
            br = bu_ref[p, t * nb:(t + 1) * nb, 0:ns]
            bi = bu_ref[p, t * nb:(t + 1) * nb, ns:2 * ns]
            sr, si = lr * sr - li * si + br, lr * si + li * sr + bi
            bu_ref[p, t * nb:(t + 1) * nb, 0:ns] = sr
            bu_ref[p, t * nb:(t + 1) * nb, ns:2 * ns] = si
        sre_ref[p] = sr
        sim_ref[p] = si
        y = jnp.dot(bu_ref[p].astype(BF16), wc_ref[p], preferred_element_type=F32)
        y = y + d_ref[:, p * cb:(p + 1) * cb] * u_ref[:, p * cb:(p + 1) * cb]
        o_ref[:, p * cb:(p + 1) * cb] = jax.nn.gelu(y).astype(o_ref.dtype)


def s5_scan(u, wb, lre, lim, wc, d_skip, nb, steps, parts):
    n, d = u.shape
    j, cb, ns2 = wb.shape
    ns = ns2 // 2
    rows = steps * nb
    jp = j // parts
    wb = wb.reshape(jp, parts, cb, ns2)
    wc = wc.reshape(jp, parts, ns2, cb)
    lre = lre.reshape(jp, parts, 1, ns)
    lim = lim.reshape(jp, parts, 1, ns)
    return pl.pallas_call(
        functools.partial(_s5_kernel, steps=steps, nb=nb, ns=ns, cb=cb, parts=parts),
        out_shape=jax.ShapeDtypeStruct((n, d), BF16),
        grid=(jp, n // rows),
        in_specs=[pl.BlockSpec((rows, parts * cb), lambda jj, i: (i, jj)),
                  pl.BlockSpec((None, parts, cb, ns2), lambda jj, i: (jj, 0, 0, 0)),
                  pl.BlockSpec((None, parts, 1, ns), lambda jj, i: (jj, 0, 0, 0)),
                  pl.BlockSpec((None, parts, 1, ns), lambda jj, i: (jj, 0, 0, 0)),
                  pl.BlockSpec((None, parts, ns2, cb), lambda jj, i: (jj, 0, 0, 0)),
                  pl.BlockSpec((1, parts * cb), lambda jj, i: (0, jj))],
        out_specs=pl.BlockSpec((rows, parts * cb), lambda jj, i: (i, jj)),
        scratch_shapes=[pltpu.VMEM((parts, rows, ns2), F32),
                        pltpu.VMEM((parts, nb, ns), F32),
                        pltpu.VMEM((parts, nb, ns), F32)],
        compiler_params=_cparams(("parallel", "arbitrary")),
        name="s5_scan",
    )(u, wb, lre, lim, wc, d_skip.reshape(1, d))


def _glu_kernel(y_ref, wa_ref, wb_ref, h_ref, o_ref):
    y = y_ref[...]
    a = jnp.dot(y, wa_ref[...], preferred_element_type=F32)
    b = jnp.dot(y, wb_ref[...], preferred_element_type=F32)
    o_ref[...] = h_ref[...] + a * jax.nn.sigmoid(b)


def glu_residual(y, wa, wb, h, rows, cols):
    n, d = y.shape
    dn = wa.shape[1]
    return pl.pallas_call(
        _glu_kernel,
        out_shape=jax.ShapeDtypeStruct((n, dn), F32),
        grid=(dn // cols, n // rows),
        in_specs=[pl.BlockSpec((rows, d), lambda c, i: (i, 0)),
                  pl.BlockSpec((d, cols), lambda c, i: (0, c)),
                  pl.BlockSpec((d, cols), lambda c, i: (0, c)),
                  pl.BlockSpec((rows, cols), lambda c, i: (i, c))],
        out_specs=pl.BlockSpec((rows, cols), lambda c, i: (i, c)),
        compiler_params=_cparams(("parallel", "parallel")),
        name="glu_residual",
    )(y, wa, wb, h)


def _mm_res_kernel(x_ref, w_ref, h_ref, o_ref):
    o_ref[...] = h_ref[...] + jnp.dot(x_ref[...], w_ref[...], preferred_element_type=F32)


def matmul_residual(x, w, h, rows, cols):
    n, d = x.shape
    dn = w.shape[1]
    return pl.pallas_call(
        _mm_res_kernel,
        out_shape=jax.ShapeDtypeStruct((n, dn), F32),
        grid=(dn // cols, n // rows),
        in_specs=[pl.BlockSpec((rows, d), lambda c, i: (i, 0)),
                  pl.BlockSpec((d, cols), lambda c, i: (0, c)),
                  pl.BlockSpec((rows, cols), lambda c, i: (i, c))],
        out_specs=pl.BlockSpec((rows, cols), lambda c, i: (i, c)),
        compiler_params=_cparams(("parallel", "parallel")),
        name="matmul_residual",
    )(x, w, h)


def _conv_in_kernel(u_ref, wbg_ref, wcg_ref, wv_ref, cw_ref, o_ref, z_ref, *, rows, nb, width):
    halo = (width - 1) * nb

    @pl.when(pl.program_id(1) == 0)
    def _():
        z_ref[0:halo, :] = jnp.zeros((halo, z_ref.shape[1]), F32)

    u = u_ref[...]
    bg = jnp.dot(u, wbg_ref[...], preferred_element_type=F32)
    cg = jnp.dot(u, wcg_ref[...], preferred_element_type=F32)
    v = jnp.dot(u, wv_ref[...], preferred_element_type=F32)
    z_ref[halo:halo + rows, :] = cg * v
    y = cw_ref[0:1, :] * z_ref[0:rows, :]
    for k in range(1, width):
        y = y + cw_ref[k:k + 1, :] * z_ref[k * nb:k * nb + rows, :]
    o_ref[...] = (bg * y).astype(o_ref.dtype)
    z_ref[0:halo, :] = z_ref[rows:rows + halo, :]


def conv_in(u, w_in, conv_w, nb, rows, cols):
    n, d = u.shape
    width = conv_w.shape[0]
    nc = d // cols
    halo = (width - 1) * nb
    return pl.pallas_call(
        functools.partial(_conv_in_kernel, rows=rows, nb=nb, width=width),
        out_shape=jax.ShapeDtypeStruct((n, d), BF16),
        grid=(nc, n // rows),
        in_specs=[pl.BlockSpec((rows, d), lambda c, i: (i, 0)),
                  pl.BlockSpec((d, cols), lambda c, i: (0, c)),
                  pl.BlockSpec((d, cols), lambda c, i: (0, c + nc)),
                  pl.BlockSpec((d, cols), lambda c, i: (0, c + 2 * nc)),
                  pl.BlockSpec((width, cols), lambda c, i: (0, c))],
        out_specs=pl.BlockSpec((rows, cols), lambda c, i: (i, c)),
        scratch_shapes=[pltpu.VMEM((rows + halo, cols), F32)],
        compiler_params=_cparams(("parallel", "arbitrary")),
        name="conv_in",
    )(u, w_in, w_in, w_in, conv_w)


def _router_kernel(h_ref, g_ref, whi_ref, wlo_ref, tri_ref, xn_ref, mf_ref, mi_ref, cnt_ref,
                   *, rows, d, n_groups, epg):
    @pl.when(pl.program_id(0) == 0)
    def _():
        cnt_ref[...] = jnp.zeros_like(cnt_ref)

    xn = _rms(h_ref[...], g_ref[...])
    x_hi = xn.astype(BF16)
    x_hi32 = x_hi.astype(F32)

    bits = lax.bitcast_convert_type(x_hi32, U32)
    words = d // LANES // 2
    for k in range(words):
        hi = bits[:, (2 * k) * LANES:(2 * k + 1) * LANES]
        lo = bits[:, (2 * k + 1) * LANES:(2 * k + 2) * LANES]
        xn_ref[pl.ds(k, rows, stride=words), :] = hi | (lo >> 16)

    x_lo = (xn - x_hi32).astype(BF16)
    w_hi = whi_ref[...]
    logits = (jnp.dot(x_hi, w_hi, preferred_element_type=F32)
              + jnp.dot(x_lo, w_hi, preferred_element_type=F32)
              + jnp.dot(x_hi, wlo_ref[...], preferred_element_type=F32))

    lane = lax.broadcasted_iota(I32, (rows, LANES), 1).astype(F32)
    neg = jnp.float32(-jnp.inf)
    big = jnp.float32(LANES)

    def first_max(vals, mask):
        m = jnp.max(jnp.where(mask, vals, neg), axis=-1, keepdims=True)
        idx = jnp.min(jnp.where(mask & (vals == m), lane, big), axis=-1, keepdims=True)
        return m, idx

    gmask = lane < n_groups
    gmax, gidx = first_max(logits, gmask)
    gsum = jnp.sum(jnp.where(gmask, jnp.exp(logits - gmax), 0.0), axis=-1, keepdims=True)
    g_w = 1.0 / gsum
    lo = n_groups + gidx * epg
    emask = (lane >= lo) & (lane < lo + epg)
    t1, i1 = first_max(logits, emask)
    t2, i2 = first_max(logits, emask & (lane != i1))
    e2 = jnp.exp(t2 - t1)
    p1 = 1.0 / (1.0 + e2)
    p2 = e2 / (1.0 + e2)
    eid1 = i1 - n_groups
    eid2 = i2 - n_groups

    onehot = ((lane == eid1) | (lane == eid2))
    oh = jnp.where(onehot, 1.0, 0.0).astype(BF16)
    before = jnp.dot(tri_ref[...], oh, preferred_element_type=F32) + cnt_ref[...]
    r1 = jnp.sum(jnp.where(lane == eid1, before, 0.0), axis=-1, keepdims=True)
    r2 = jnp.sum(jnp.where(lane == eid2, before, 0.0), axis=-1, keepdims=True)
    cnt_ref[...] = cnt_ref[...] + jnp.sum(oh.astype(F32), axis=0, keepdims=True)

    mf_ref[...] = jnp.where(lane == 0, g_w * p1, jnp.where(lane == 1, g_w * p2, 0.0))
    mi_ref[...] = jnp.where(lane == 0, eid1, jnp.where(lane == 1, eid2,
                            jnp.where(lane == 2, r1, jnp.where(lane == 3, r2, 0.0)))).astype(I32)


def router(h, g, w_group, w_expert, rows):
    n, d = h.shape
    n_groups = w_group.shape[1]
    epg = w_expert.shape[2]
    n_exp = n_groups * epg
    assert n_groups + n_exp <= LANES
    words = d // LANES // 2
    w = jnp.concatenate([w_group.astype(F32),
                         jnp.transpose(w_expert.astype(F32), (1, 0, 2)).reshape(d, n_exp)], axis=1)
    w = jnp.pad(w, ((0, 0), (0, LANES - n_groups - n_exp)))
    w_hi = w.astype(BF16)
    w_lo = (w - w_hi.astype(F32)).astype(BF16)
    tri = (lax.broadcasted_iota(I32, (rows, rows), 1) < lax.broadcasted_iota(I32, (rows, rows), 0)).astype(BF16)
    xn, mf, mi, cnt = pl.pallas_call(
        functools.partial(_router_kernel, rows=rows, d=d, n_groups=n_groups, epg=epg),
        out_shape=(jax.ShapeDtypeStruct((n * words, LANES), U32),
                   jax.ShapeDtypeStruct((n, LANES), F32),
                   jax.ShapeDtypeStruct((n, LANES), I32),
                   jax.ShapeDtypeStruct((1, LANES), F32)),
        grid=(n // rows,),
        in_specs=[pl.BlockSpec((rows, d), lambda i: (i, 0)),
                  pl.BlockSpec((1, d), lambda i: (0, 0)),
                  pl.BlockSpec((d, LANES), lambda i: (0, 0)),
                  pl.BlockSpec((d, LANES), lambda i: (0, 0)),
                  pl.BlockSpec((rows, rows), lambda i: (0, 0))],
        out_specs=(pl.BlockSpec((rows * words, LANES), lambda i: (i, 0)),
                   pl.BlockSpec((rows, LANES), lambda i: (i, 0)),
                   pl.BlockSpec((rows, LANES), lambda i: (i, 0)),
                   pl.BlockSpec((1, LANES), lambda i: (0, 0))),
        compiler_params=_cparams(("arbitrary",)),
        name="moe_router",
    )(h, g.reshape(1, d), w_hi, w_lo, tri)
    return xn, mf, mi, cnt


def _inverse_kernel(dest_ref, lo_ref, hi_ref, inv_ref, *, n_asg, n_fill):
    def fill_range(e, c):
        def fill(r, c2):
            inv_ref[r] = 0
            return c2

        lax.fori_loop(lo_ref[e], hi_ref[e], fill, 0)
        return c

    lax.fori_loop(0, n_fill, fill_range, 0)

    def scatter(t, c):
        for k in range(TOP_K):
            inv_ref[dest_ref[TOP_K * t + k]] = t
        return c

    lax.fori_loop(0, n_asg // TOP_K, scatter, 0, unroll=DMA_UNROLL)


def moe_inverse(dest_flat, fill_lo, fill_hi, n_rows):
    smem = pl.BlockSpec(memory_space=pltpu.SMEM)
    return pl.pallas_call(
        functools.partial(_inverse_kernel, n_asg=dest_flat.shape[0], n_fill=fill_lo.shape[0]),
        out_shape=jax.ShapeDtypeStruct((n_rows,), I32),
        in_specs=[smem, smem, smem],
        out_specs=smem,
        name="moe_inverse",
    )(dest_flat, fill_lo, fill_hi)


def _gather_pitch(words):
    return words + 2


def _moe_ffn_kernel(be_ref, nu_ref, first_ref, nxt_ref, par_ref, invc_ref, invn_ref, xn_hbm,
                    wg_hbm, wu_hbm, wd_hbm, ys_ref,
                    wgs_ref, wus_ref, wds_ref, wgb_ref, wub_ref, wdb_ref, gbuf_ref, sem, gsem,
                    *, layer, bm, chunks):
    i = pl.program_id(0)
    words = chunks // 2
    pitch = _gather_pitch(words)
    slot_rows = bm * pitch

    def weight_copies(e, s):
        return (pltpu.make_async_copy(wg_hbm.at[layer, e], wgs_ref.at[s], sem.at[s, 0]),
                pltpu.make_async_copy(wu_hbm.at[layer, e], wus_ref.at[s], sem.at[s, 1]),
                pltpu.make_async_copy(wd_hbm.at[layer, e], wds_ref.at[s], sem.at[s, 2]))

    def start_gather(inv_ref, slot):
        def body(r, c):
            tok = inv_ref[0, 0, r]
            pltpu.make_async_copy(xn_hbm.at[pl.ds(pl.multiple_of(tok * words, words), words)],
                                  gbuf_ref.at[pl.ds(slot * slot_rows + r * pitch, words)], gsem.at[slot]).start()
            return c

        lax.fori_loop(0, bm, body, 0, unroll=DMA_UNROLL)

    def wait_gather(slot):
        pltpu.make_async_copy(xn_hbm.at[pl.ds(0, bm * words)], gbuf_ref.at[pl.ds(slot * slot_rows, bm * words)],
                              gsem.at[slot]).wait()

    @pl.when(i == 0)
    def _():
        for cp in weight_copies(be_ref[0], 0):
            cp.start()
        start_gather(invc_ref, 0)

    for s in range(2):
        @pl.when((first_ref[i] == 1) & (par_ref[i] == s))
        def _():
            for cp in weight_copies(be_ref[i], s):
                cp.wait()

            @pl.when(nxt_ref[i] >= 0)
            def _():
                for cp in weight_copies(nxt_ref[i], 1 - s):
                    cp.start()

            wgb_ref[...] = wgs_ref[s].astype(BF16)
            wub_ref[...] = wus_ref[s].astype(BF16)
            wdb_ref[...] = wds_ref[s].astype(BF16)

    for slot in range(2):
        @pl.when((i < nu_ref[0]) & (i % 2 == slot))
        def _():
            @pl.when(i + 1 < nu_ref[0])
            def _():
                start_gather(invn_ref, 1 - slot)

            wait_gather(slot)
            halves = []
            for k in range(words):
                w = gbuf_ref[pl.ds(slot * slot_rows + k, bm, stride=pitch), :]
                halves.append(lax.bitcast_convert_type(w & jnp.uint32(0xFFFF0000), F32).astype(BF16))
                halves.append(lax.bitcast_convert_type(w << 16, F32).astype(BF16))
            x = jnp.concatenate(halves, axis=-1)
            gt = jnp.dot(x, wgb_ref[...], preferred_element_type=F32)
            up = jnp.dot(x, wub_ref[...], preferred_element_type=F32)
            mid = (jax.nn.silu(gt) * up).astype(BF16)
            y = jnp.dot(mid, wdb_ref[...], preferred_element_type=F32)
            for c in range(chunks):
                ys_ref[pl.ds(c, bm, stride=chunks), :] = y[:, c * LANES:(c + 1) * LANES]

    @pl.when(i >= nu_ref[0])
    def _():
        ys_ref[...] = jnp.zeros_like(ys_ref)


def moe_ffn(blk_expert, n_used, inv, xn_lin, w_gate, w_up, w_down, layer, bm, chunks):
    words = chunks // 2
    n_blocks = inv.shape[0] // bm
    _, _, d, de = w_gate.shape

    ar = jnp.arange(n_blocks, dtype=I32)
    first = jnp.concatenate([jnp.ones((1,), I32), (blk_expert[1:] != blk_expert[:-1]).astype(I32)])
    parity = (jnp.cumsum(first) - 1) % 2
    start_pos = jnp.where(first == 1, ar, n_blocks)
    next_start = jnp.concatenate([lax.cummin(start_pos, reverse=True)[1:], jnp.full((1,), n_blocks, I32)])
    nxt = jnp.where(next_start < n_blocks, blk_expert[jnp.minimum(next_start, n_blocks - 1)], -1).astype(I32)

    inv3 = inv.reshape(n_blocks, 1, bm)
    any_spec = pl.BlockSpec(memory_space=pl.ANY)
    return pl.pallas_call(
        functools.partial(_moe_ffn_kernel, layer=layer, bm=bm, chunks=chunks),
        out_shape=jax.ShapeDtypeStruct((n_blocks * bm * chunks, LANES), F32),
        grid_spec=pltpu.PrefetchScalarGridSpec(
            num_scalar_prefetch=5,
            grid=(n_blocks,),
            in_specs=[pl.BlockSpec((1, 1, bm), lambda i, *_: (i, 0, 0), memory_space=pltpu.SMEM),
                      pl.BlockSpec((1, 1, bm), lambda i, *_: (jnp.minimum(i + 1, n_blocks - 1), 0, 0),
                                   memory_space=pltpu.SMEM),
                      any_spec, any_spec, any_spec, any_spec],
            out_specs=pl.BlockSpec((bm * chunks, LANES), lambda i, *_: (i, 0)),
            scratch_shapes=[pltpu.VMEM((2, d, de), F32), pltpu.VMEM((2, d, de), F32), pltpu.VMEM((2, de, d), F32),
                            pltpu.VMEM((d, de), BF16), pltpu.VMEM((d, de), BF16), pltpu.VMEM((de, d), BF16),
                            pltpu.VMEM((2 * bm * _gather_pitch(words), LANES), U32),
                            pltpu.SemaphoreType.DMA((2, 3)), pltpu.SemaphoreType.DMA((2,))]),
        compiler_params=_cparams(("arbitrary",)),
        name="moe_ffn",
    )(blk_expert, n_used, first, nxt, parity.astype(I32), inv3, inv3, xn_lin, w_gate, w_up, w_down)


def _slab_pitch(chunks):
    return chunks + 2


def _combine_kernel(dc_ref, dn_ref, ys_ref, h_ref, mf_ref, g_ref, *rest, toks, chunks, n_tiles, nb):
    if nb:
        fo_ref, buf_ref, ub_ref, sem = rest
    else:
        ho_ref, uo_ref, buf_ref, sem = rest
    i = pl.program_id(0)
    slab = _slab_pitch(chunks)
    slot_rows = toks * TOP_K * slab
    stride = TOP_K * slab

    def start_tile(dest_ref, slot):
        def body(r, c):
            for k in range(TOP_K):
                src = dest_ref[0, 0, TOP_K * r + k]
                pltpu.make_async_copy(ys_ref.at[pl.ds(pl.multiple_of(src * chunks, chunks), chunks)],
                                      buf_ref.at[pl.ds(slot * slot_rows + (r * TOP_K + k) * slab, chunks)],
                                      sem.at[slot]).start()
            return c

        lax.fori_loop(0, toks, body, 0, unroll=DMA_UNROLL)

    def wait_tile(slot):
        rows = toks * TOP_K * chunks
        pltpu.make_async_copy(ys_ref.at[pl.ds(0, rows)], buf_ref.at[pl.ds(slot * slot_rows, rows)],
                              sem.at[slot]).wait()

    def run(slot):
        if slot == 0:
            @pl.when(i == 0)
            def _():
                start_tile(dc_ref, 0)

        @pl.when(i + 1 < n_tiles)
        def _():
            start_tile(dn_ref, 1 - slot)

        wait_tile(slot)
        base = slot * slot_rows
        y0 = jnp.concatenate([buf_ref[pl.ds(base + c, toks, stride=stride), :] for c in range(chunks)], axis=-1)
        y1 = jnp.concatenate([buf_ref[pl.ds(base + slab + c, toks, stride=stride), :] for c in range(chunks)],
                             axis=-1)
        mf = mf_ref[...]
        h = h_ref[...] + mf[:, 0:1] * y0 + mf[:, 1:2] * y1
        if nb:
            u = _rms(h, g_ref[...])
            for c in range(chunks):
                ub_ref[c] = u[:, c * LANES:(c + 1) * LANES]
            for b in range(nb):
                fo_ref[b] = jnp.concatenate(
                    [ub_ref[c, pl.ds(b, toks // nb, stride=nb), :] for c in range(chunks)], axis=-1)
        else:
            ho_ref[...] = h
            uo_ref[...] = _rms(h, g_ref[...]).astype(uo_ref.dtype)

    @pl.when(i % 2 == 0)
    def _():
        run(0)

    @pl.when(i % 2 == 1)
    def _():
        run(1)


def moe_combine(dest, ys_lin, h, mf, g_next, u_dtype, toks, chunks, nb=0):
    n, d = h.shape
    n_tiles = n // toks
    row_spec = pl.BlockSpec((toks, d), lambda i: (i, 0))
    gather_buf = pltpu.VMEM((2 * toks * TOP_K * _slab_pitch(chunks), LANES), F32)
    if nb:
        out_shape = jax.ShapeDtypeStruct((nb, n // nb, d), F32)
        out_specs = pl.BlockSpec((nb, toks // nb, d), lambda i: (0, i, 0))
        scratch = [gather_buf, pltpu.VMEM((chunks, toks, LANES), F32), pltpu.SemaphoreType.DMA((2,))]
    else:
        out_shape = (jax.ShapeDtypeStruct((n, d), F32), jax.ShapeDtypeStruct((n, d), u_dtype))
        out_specs = (row_spec, row_spec)
        scratch = [gather_buf, pltpu.SemaphoreType.DMA((2,))]
    return pl.pallas_call(
        functools.partial(_combine_kernel, toks=toks, chunks=chunks, n_tiles=n_tiles, nb=nb),
        out_shape=out_shape,
        grid=(n_tiles,),
        in_specs=[pl.BlockSpec((1, 1, TOP_K * toks), lambda i: (i, 0, 0), memory_space=pltpu.SMEM),
                  pl.BlockSpec((1, 1, TOP_K * toks), lambda i: (jnp.minimum(i + 1, n_tiles - 1), 0, 0),
                               memory_space=pltpu.SMEM),
                  pl.BlockSpec(memory_space=pl.ANY),
                  row_spec,
                  pl.BlockSpec((toks, LANES), lambda i: (i, 0)),
                  pl.BlockSpec((1, d), lambda i: (0, 0))],
        out_specs=out_specs,
        scratch_shapes=scratch,
        compiler_params=_cparams(("arbitrary",)),
        name="moe_combine",
    )(dest, dest, ys_lin, h, mf, g_next.reshape(1, d))


def hier_moe_layer(h, g_ffn, w_group, w_expert, w_gate, w_up, w_down, layer, g_next, u_dtype,
                   *, rows, bm, toks_c, final_nb=0):
    n, d = h.shape
    chunks = d // LANES
    n_exp = w_gate.shape[1]
    xn_lin, mf, mi, cnt = router(h, g_ffn, w_group, w_expert, rows)

    counts = cnt[0, :n_exp].astype(I32)
    padded = (counts + bm - 1) // bm * bm
    pad_end = jnp.cumsum(padded)
    pad_start = pad_end - padded
    n_rows = n * TOP_K + n_exp * bm
    n_blocks = n_rows // bm
    blk_start = jnp.arange(n_blocks, dtype=I32) * bm
    blk_expert = jnp.sum((blk_start[:, None] >= pad_end[None, :]).astype(I32), axis=1)
    blk_expert = jnp.minimum(blk_expert, n_exp - 1).astype(I32)
    n_used = (pad_end[-1:] // bm).astype(I32)
    last_e = blk_expert[jnp.maximum(n_used[0] - 1, 0)]
    blk_expert = jnp.where(jnp.arange(n_blocks) < n_used[0], blk_expert, last_e)

    eid = mi[:, 0:TOP_K]
    seg = jnp.sum(jnp.where(eid[:, :, None] == jnp.arange(n_exp, dtype=I32), pad_start, 0), axis=-1)
    dest = (seg + mi[:, TOP_K:2 * TOP_K]).astype(I32)
    fill_lo = jnp.concatenate([pad_start + counts, pad_end[-1:]]).astype(I32)
    fill_hi = jnp.concatenate([pad_end, jnp.full((1,), n_rows, I32)]).astype(I32)
    inv = moe_inverse(dest.reshape(n * TOP_K), fill_lo, fill_hi, n_rows)
    ys_lin = moe_ffn(blk_expert, n_used, inv, xn_lin, w_gate, w_up, w_down, layer, bm, chunks)
    dest_c = dest.reshape(n // toks_c, 1, TOP_K * toks_c)
    if final_nb:
        return None, moe_combine(dest_c, ys_lin, h, mf, g_next, u_dtype, toks_c, chunks, final_nb)
    return moe_combine(dest_c, ys_lin, h, mf, g_next, u_dtype, toks_c, chunks)


def trunk(x, norm_mix, norm_ffn, norm_final, ssm_a_re, ssm_a_im, ssm_log_dt, ssm_b_re, ssm_b_im,
          ssm_c_re, ssm_c_im, ssm_d, ssm_w_glu_a, ssm_w_glu_b, conv_w_in, conv_w, conv_w_out,
          moe_w_group_router, moe_w_expert_router, moe_w_gate, moe_w_up, moe_w_down,
          *, rows, steps, cols, cols_out, bm, toks_c, cblock, cparts):
    bsz, seq, d = x.shape
    depth = norm_mix.shape[0]
    hh = ssm_b_re.shape[3]
    gb = cblock // hh

    h, u = norm_in(x, norm_mix[0], steps)
    for i in range(depth):
        j = i // 2
        if i % 2 == 0:
            bb_re, bb_im, l_re, l_im = ssm_prep(ssm_a_re[j], ssm_a_im[j], ssm_log_dt[j], ssm_b_re[j], ssm_b_im[j])
            wb, wc, lre, lim = _ssm_block_weights(bb_re, bb_im, ssm_c_re[j], ssm_c_im[j], l_re, l_im, gb)
            yg = s5_scan(u, wb, lre, lim, wc, ssm_d[j].astype(F32), bsz, steps, cparts)
            h = glu_residual(yg, ssm_w_glu_a[j].astype(BF16), ssm_w_glu_b[j].astype(BF16), h, rows, cols_out)
        else:
            gy = conv_in(u, conv_w_in[j].astype(BF16), conv_w[j].astype(F32), bsz, rows, cols)
            h = matmul_residual(gy, conv_w_out[j].astype(BF16), h, rows, cols_out)
        last = i == depth - 1
        g_next = norm_final if last else norm_mix[i + 1]
        u_dtype = F32 if (last or (i + 1) % 2 == 0) else BF16
        h, u = hier_moe_layer(h, norm_ffn[i], moe_w_group_router[i], moe_w_expert_router[i],
                              moe_w_gate, moe_w_up, moe_w_down, i, g_next, u_dtype,
                              rows=rows, bm=bm, toks_c=toks_c, final_nb=bsz if last else 0)
    return u


def kernel(x, norm_mix, norm_ffn, norm_final, ssm_a_re, ssm_a_im, ssm_log_dt, ssm_b_re, ssm_b_im, ssm_c_re, ssm_c_im, ssm_d, ssm_w_glu_a, ssm_w_glu_b, conv_w_in, conv_w, conv_w_out, moe_w_group_router, moe_w_expert_router, moe_w_gate, moe_w_up, moe_w_down):
    return trunk(x, norm_mix, norm_ffn, norm_final, ssm_a_re, ssm_a_im, ssm_log_dt, ssm_b_re, ssm_b_im,
                 ssm_c_re, ssm_c_im, ssm_d, ssm_w_glu_a, ssm_w_glu_b, conv_w_in, conv_w, conv_w_out,
                 moe_w_group_router, moe_w_expert_router, moe_w_gate, moe_w_up, moe_w_down,
                 rows=512, steps=64, cols=512, cols_out=1024, bm=256, toks_c=128, cblock=256, cparts=4)
```

```python
import functools

import jax
import jax.numpy as jnp
from jax import lax
from jax.experimental import pallas as pl
from jax.experimental.pallas import tpu as pltpu

F32 = jnp.float32
BF16 = jnp.bfloat16
I32 = jnp.int32
U32 = jnp.uint32

LANES = 128
SUBLANES = 8
VMEM_LIMIT = 56 * 1024 * 1024
RMS_EPS = 1e-6
TOP_K = 2
DMA_UNROLL = 8


def _cparams(sem):
    return pltpu.CompilerParams(dimension_semantics=sem, vmem_limit_bytes=VMEM_LIMIT)


def _rms(h, g):
    ms = jnp.mean(h * h, axis=-1, keepdims=True)
    return h * lax.rsqrt(ms + RMS_EPS) * g


def _pack_rows(x_bf16_as_f32, k):
    bits = lax.bitcast_convert_type(x_bf16_as_f32[:, (2 * k) * LANES:(2 * k + 2) * LANES], U32)
    return bits[:, :LANES] | (bits[:, LANES:] >> 16)


def _unpack_words(w):
    return (lax.bitcast_convert_type(w & jnp.uint32(0xFFFF0000), F32),
            lax.bitcast_convert_type(w << 16, F32))


def _norm_in_kernel(x_ref, g_ref, h_ref, u_ref, t_ref, *, nb, steps):
    chunks = t_ref.shape[0]
    for b in range(nb):
        xb = x_ref[b]
        for c in range(chunks):
            t_ref[c, pl.ds(b, steps, stride=nb), :] = xb[:, c * LANES:(c + 1) * LANES]
    h = jnp.concatenate([t_ref[c] for c in range(chunks)], axis=-1)
    h_ref[...] = h
    u_ref[...] = _rms(h, g_ref[...])


def norm_in(x, g, steps):
    nb, seq, d = x.shape
    rows = steps * nb
    row_spec = pl.BlockSpec((rows, d), lambda i: (i, 0))
    return pl.pallas_call(
        functools.partial(_norm_in_kernel, nb=nb, steps=steps),
        out_shape=(jax.ShapeDtypeStruct((nb * seq, d), F32), jax.ShapeDtypeStruct((nb * seq, d), F32)),
        grid=(seq // steps,),
        in_specs=[pl.BlockSpec((nb, steps, d), lambda i: (0, i, 0)),
                  pl.BlockSpec((1, d), lambda i: (0, 0))],
        out_specs=(row_spec, row_spec),
        scratch_shapes=[pltpu.VMEM((d // LANES, rows, LANES), F32)],
        compiler_params=_cparams(("parallel",)),
        name="norm_in",
    )(x, g.reshape(1, d))


def _zoh(a_re, a_im, log_dt):
    dt = jnp.exp(log_dt)
    mag = jnp.exp(a_re * dt)
    return mag * jnp.cos(a_im * dt), mag * jnp.sin(a_im * dt)


def _ssm_prep_kernel(are_ref, aim_ref, ldt_ref, bre_ref, bim_ref,
                     are2_ref, aim2_ref, ldt2_ref,
                     bbre_ref, bbim_ref, lre_ref, lim_ref):
    a_re = are_ref[...]
    a_im = aim_ref[...]
    lb_re, lb_im = _zoh(a_re, a_im, ldt_ref[...])
    den = a_re * a_re + a_im * a_im
    nr = lb_re - 1.0
    ni = lb_im
    coef_re = (nr * a_re + ni * a_im) / den
    coef_im = (ni * a_re - nr * a_im) / den
    br = bre_ref[...]
    bi = bim_ref[...]
    bbre_ref[...] = coef_re * br - coef_im * bi
    bbim_ref[...] = coef_re * bi + coef_im * br
    l_re, l_im = _zoh(are2_ref[...], aim2_ref[...], ldt2_ref[...])
    lre_ref[...] = l_re
    lim_ref[...] = l_im


def ssm_prep(a_re, a_im, log_dt, b_re, b_im):
    g, p, hh = b_re.shape
    rep = lambda a: jnp.repeat(a.astype(F32), hh, axis=1)
    ldt2 = jnp.broadcast_to(log_dt.astype(F32)[:, None], (g, p))
    outs = pl.pallas_call(
        _ssm_prep_kernel,
        out_shape=(jax.ShapeDtypeStruct((g, p * hh), F32), jax.ShapeDtypeStruct((g, p * hh), F32),
                   jax.ShapeDtypeStruct((g, p), F32), jax.ShapeDtypeStruct((g, p), F32)),
        name="ssm_prep",
    )(rep(a_re), rep(a_im), rep(ldt2), b_re.astype(F32).reshape(g, p * hh), b_im.astype(F32).reshape(g, p * hh),
      a_re.astype(F32), a_im.astype(F32), ldt2)
    bb_re, bb_im, l_re, l_im = outs
    return bb_re.reshape(g, p, hh), bb_im.reshape(g, p, hh), l_re, l_im


def _ssm_block_weights(bb_re, bb_im, c_re, c_im, l_re, l_im, gb):
    g, p, hh = bb_re.shape
    j = g // gb
    eye = jnp.eye(gb, dtype=F32)

    def emb_b(bb):
        return jnp.einsum('jgph,gk->jghkp', bb.reshape(j, gb, p, hh), eye).reshape(j, gb * hh, gb * p)

    def emb_c(c):
        return jnp.einsum('jghp,gk->jkpgh', c.reshape(j, gb, hh, p), eye).reshape(j, gb * p, gb * hh)

    wb = jnp.concatenate([emb_b(bb_re), emb_b(bb_im)], axis=2).astype(BF16)
    wc = jnp.concatenate([emb_c(c_re.astype(F32)), -emb_c(c_im.astype(F32))], axis=1).astype(BF16)
    lre = l_re.reshape(j, 1, gb * p)
    lim = l_im.reshape(j, 1, gb * p)
    return wb, wc, lre, lim


def _s5_kernel(u_ref, wb_ref, lre_ref, lim_ref, wc_ref, d_ref, o_ref,
               bu_ref, sre_ref, sim_ref, *, steps, nb, ns, cb, parts):
    @pl.when(pl.program_id(1) == 0)
    def _():
        sre_ref[...] = jnp.zeros_like(sre_ref)
        sim_ref[...] = jnp.zeros_like(sim_ref)

    def project_in(p):
        u = u_ref[:, p * cb:(p + 1) * cb]
        bu_ref[p] = jnp.dot(u.astype(BF16), wb_ref[p], preferred_element_type=F32)

    project_in(0)
    for p in range(parts):
        if p + 1 < parts:
            project_in(p + 1)
        lr = jnp.broadcast_to(lre_ref[p], (nb, ns))
        li = jnp.broadcast_to(lim_ref[p], (nb, ns))
        sr = sre_ref[p]
        si = sim_ref[p]
        for t in range(steps):
            br = bu_ref[p, t * nb:(t + 1) * nb, 0:ns]
            bi = bu_ref[p, t * nb:(t + 1) * nb, ns:2 * ns]
            sr, si = lr * sr - li * si + br, lr * si + li * sr + bi
            bu_ref[p, t * nb:(t + 1) * nb, 0:ns] = sr
            bu_ref[p, t * nb:(t + 1) * nb, ns:2 * ns] = si
        sre_ref[p] = sr
        sim_ref[p] = si
        y = jnp.dot(bu_ref[p].astype(BF16), wc_ref[p], preferred_element_type=F32)
        y = y + d_ref[:, p * cb:(p + 1) * cb] * u_ref[:, p * cb:(p + 1) * cb]
        o_ref[:, p * cb:(p + 1) * cb] = jax.nn.gelu(y).astype(o_ref.dtype)


def s5_scan(u, wb, lre, lim, wc, d_skip, nb, steps, parts):
    n, d = u.shape
    j, cb, ns2 = wb.shape
    ns = ns2 // 2
    rows = steps * nb
    jp = j // parts
    wb = wb.reshape(jp, parts, cb, ns2)
    wc = wc.reshape(jp, parts, ns2, cb)
    lre = lre.reshape(jp, parts, 1, ns)
    lim = lim.reshape(jp, parts, 1, ns)
    return pl.pallas_call(
        functools.partial(_s5_kernel, steps=steps, nb=nb, ns=ns, cb=cb, parts=parts),
        out_shape=jax.ShapeDtypeStruct((n, d), BF16),
        grid=(jp, n // rows),
        in_specs=[pl.BlockSpec((rows, parts * cb), lambda jj, i: (i, jj)),
                  pl.BlockSpec((None, parts, cb, ns2), lambda jj, i: (jj, 0, 0, 0)),
                  pl.BlockSpec((None, parts, 1, ns), lambda jj, i: (jj, 0, 0, 0)),
                  pl.BlockSpec((None, parts, 1, ns), lambda jj, i: (jj, 0, 0, 0)),
                  pl.BlockSpec((None, parts, ns2, cb), lambda jj, i: (jj, 0, 0, 0)),
                  pl.BlockSpec((1, parts * cb), lambda jj, i: (0, jj))],
        out_specs=pl.BlockSpec((rows, parts * cb), lambda jj, i: (i, jj)),
        scratch_shapes=[pltpu.VMEM((parts, rows, ns2), F32),
                        pltpu.VMEM((parts, nb, ns), F32),
                        pltpu.VMEM((parts, nb, ns), F32)],
        compiler_params=_cparams(("parallel", "arbitrary")),
        name="s5_scan",
    )(u, wb, lre, lim, wc, d_skip.reshape(1, d))


def _glu_kernel(y_ref, wa_ref, wb_ref, h_ref, o_ref):
    y = y_ref[...]
    a = jnp.dot(y, wa_ref[...], preferred_element_type=F32)
    b = jnp.dot(y, wb_ref[...], preferred_element_type=F32)
    o_ref[...] = h_ref[...] + a * jax.nn.sigmoid(b)


def glu_residual(y, wa, wb, h, rows, cols):
    n, d = y.shape
    dn = wa.shape[1]
    return pl.pallas_call(
        _glu_kernel,
        out_shape=jax.ShapeDtypeStruct((n, dn), F32),
        grid=(dn // cols, n // rows),
        in_specs=[pl.BlockSpec((rows, d), lambda c, i: (i, 0)),
                  pl.BlockSpec((d, cols), lambda c, i: (0, c)),
                  pl.BlockSpec((d, cols), lambda c, i: (0, c)),
                  pl.BlockSpec((rows, cols), lambda c, i: (i, c))],
        out_specs=pl.BlockSpec((rows, cols), lambda c, i: (i, c)),
        compiler_params=_cparams(("parallel", "parallel")),
        name="glu_residual",
    )(y, wa, wb, h)


def _mm_res_kernel(x_ref, w_ref, h_ref, o_ref):
    o_ref[...] = h_ref[...] + jnp.dot(x_ref[...], w_ref[...], preferred_element_type=F32)


def matmul_residual(x, w, h, rows, cols):
    n, d = x.shape
    dn = w.shape[1]
    return pl.pallas_call(
        _mm_res_kernel,
        out_shape=jax.ShapeDtypeStruct((n, dn), F32),
        grid=(dn // cols, n // rows),
        in_specs=[pl.BlockSpec((rows, d), lambda c, i: (i, 0)),
                  pl.BlockSpec((d, cols), lambda c, i: (0, c)),
                  pl.BlockSpec((rows, cols), lambda c, i: (i, c))],
        out_specs=pl.BlockSpec((rows, cols), lambda c, i: (i, c)),
        compiler_params=_cparams(("parallel", "parallel")),
        name="matmul_residual",
    )(x, w, h)


def _conv_in_kernel(u_ref, wbg_ref, wcg_ref, wv_ref, cw_ref, o_ref, z_ref, *, rows, nb, width):
    halo = (width - 1) * nb

    @pl.when(pl.program_id(1) == 0)
    def _():
        z_ref[0:halo, :] = jnp.zeros((halo, z_ref.shape[1]), F32)

    u = u_ref[...]
    bg = jnp.dot(u, wbg_ref[...], preferred_element_type=F32)
    cg = jnp.dot(u, wcg_ref[...], preferred_element_type=F32)
    v = jnp.dot(u, wv_ref[...], preferred_element_type=F32)
    z_ref[halo:halo + rows, :] = cg * v
    y = cw_ref[0:1, :] * z_ref[0:rows, :]
    for k in range(1, width):
        y = y + cw_ref[k:k + 1, :] * z_ref[k * nb:k * nb + rows, :]
    o_ref[...] = (bg * y).astype(o_ref.dtype)
    z_ref[0:halo, :] = z_ref[rows:rows + halo, :]


def conv_in(u, w_in, conv_w, nb, rows, cols):
    n, d = u.shape
    width = conv_w.shape[0]
    nc = d // cols
    halo = (width - 1) * nb
    return pl.pallas_call(
        functools.partial(_conv_in_kernel, rows=rows, nb=nb, width=width),
        out_shape=jax.ShapeDtypeStruct((n, d), BF16),
        grid=(nc, n // rows),
        in_specs=[pl.BlockSpec((rows, d), lambda c, i: (i, 0)),
                  pl.BlockSpec((d, cols), lambda c, i: (0, c)),
                  pl.BlockSpec((d, cols), lambda c, i: (0, c + nc)),
                  pl.BlockSpec((d, cols), lambda c, i: (0, c + 2 * nc)),
                  pl.BlockSpec((width, cols), lambda c, i: (0, c))],
        out_specs=pl.BlockSpec((rows, cols), lambda c, i: (i, c)),
        scratch_shapes=[pltpu.VMEM((rows + halo, cols), F32)],
        compiler_params=_cparams(("parallel", "arbitrary")),
        name="conv_in",
    )(u, w_in, w_in, w_in, conv_w)


def _router_kernel(h_ref, g_ref, whi_ref, wlo_ref, tri_ref, xn_ref, mf_ref, mi_ref, cnt_ref,
                   *, rows, d, n_groups, epg):
    @pl.when(pl.program_id(0) == 0)
    def _():
        cnt_ref[...] = jnp.zeros_like(cnt_ref)

    xn = _rms(h_ref[...], g_ref[...])
    x_hi = xn.astype(BF16)
    x_hi32 = x_hi.astype(F32)

    words = d // LANES // 2
    for k in range(words):
        xn_ref[pl.ds(k, rows, stride=words), :] = _pack_rows(x_hi32, k)

    x_lo = (xn - x_hi32).astype(BF16)
    w_hi = whi_ref[...]
    logits = (jnp.dot(x_hi, w_hi, preferred_element_type=F32)
              + jnp.dot(x_lo, w_hi, preferred_element_type=F32)
              + jnp.dot(x_hi, wlo_ref[...], preferred_element_type=F32))

    lane = lax.broadcasted_iota(I32, (rows, LANES), 1).astype(F32)
    neg = jnp.float32(-jnp.inf)
    big = jnp.float32(LANES)

    def first_max(vals, mask):
        m = jnp.max(jnp.where(mask, vals, neg), axis=-1, keepdims=True)
        idx = jnp.min(jnp.where(mask & (vals == m), lane, big), axis=-1, keepdims=True)
        return m, idx

    gmask = lane < n_groups
    gmax, gidx = first_max(logits, gmask)
    gsum = jnp.sum(jnp.where(gmask, jnp.exp(logits - gmax), 0.0), axis=-1, keepdims=True)
    g_w = 1.0 / gsum
    lo = n_groups + gidx * epg
    emask = (lane >= lo) & (lane < lo + epg)
    t1, i1 = first_max(logits, emask)
    t2, i2 = first_max(logits, emask & (lane != i1))
    e2 = jnp.exp(t2 - t1)
    p1 = 1.0 / (1.0 + e2)
    p2 = e2 / (1.0 + e2)
    eid1 = i1 - n_groups
    eid2 = i2 - n_groups

    onehot = ((lane == eid1) | (lane == eid2))
    oh = jnp.where(onehot, 1.0, 0.0).astype(BF16)
    before = jnp.dot(tri_ref[...], oh, preferred_element_type=F32) + cnt_ref[...]
    r1 = jnp.sum(jnp.where(lane == eid1, before, 0.0), axis=-1, keepdims=True)
    r2 = jnp.sum(jnp.where(lane == eid2, before, 0.0), axis=-1, keepdims=True)
    cnt_ref[...] = cnt_ref[...] + jnp.sum(oh.astype(F32), axis=0, keepdims=True)

    mf_ref[...] = jnp.where(lane == 0, g_w * p1, jnp.where(lane == 1, g_w * p2, 0.0))
    mi_ref[...] = jnp.where(lane == 0, eid1, jnp.where(lane == 1, eid2,
                            jnp.where(lane == 2, r1, jnp.where(lane == 3, r2, 0.0)))).astype(I32)


def router(h, g, w_group, w_expert, rows):
    n, d = h.shape
    n_groups = w_group.shape[1]
    epg = w_expert.shape[2]
    n_exp = n_groups * epg
    assert n_groups + n_exp <= LANES
    words = d // LANES // 2
    w = jnp.concatenate([w_group.astype(F32),
                         jnp.transpose(w_expert.astype(F32), (1, 0, 2)).reshape(d, n_exp)], axis=1)
    w = jnp.pad(w, ((0, 0), (0, LANES - n_groups - n_exp)))
    w_hi = w.astype(BF16)
    w_lo = (w - w_hi.astype(F32)).astype(BF16)
    tri = (lax.broadcasted_iota(I32, (rows, rows), 1) < lax.broadcasted_iota(I32, (rows, rows), 0)).astype(BF16)
    xn, mf, mi, cnt = pl.pallas_call(
        functools.partial(_router_kernel, rows=rows, d=d, n_groups=n_groups, epg=epg),
        out_shape=(jax.ShapeDtypeStruct((n * words, LANES), U32),
                   jax.ShapeDtypeStruct((n, LANES), F32),
                   jax.ShapeDtypeStruct((n, LANES), I32),
                   jax.ShapeDtypeStruct((1, LANES), F32)),
        grid=(n // rows,),
        in_specs=[pl.BlockSpec((rows, d), lambda i: (i, 0)),
                  pl.BlockSpec((1, d), lambda i: (0, 0)),
                  pl.BlockSpec((d, LANES), lambda i: (0, 0)),
                  pl.BlockSpec((d, LANES), lambda i: (0, 0)),
                  pl.BlockSpec((rows, rows), lambda i: (0, 0))],
        out_specs=(pl.BlockSpec((rows * words, LANES), lambda i: (i, 0)),
                   pl.BlockSpec((rows, LANES), lambda i: (i, 0)),
                   pl.BlockSpec((rows, LANES), lambda i: (i, 0)),
                   pl.BlockSpec((1, LANES), lambda i: (0, 0))),
        compiler_params=_cparams(("arbitrary",)),
        name="moe_router",
    )(h, g.reshape(1, d), w_hi, w_lo, tri)
    return xn, mf, mi, cnt


def _dispatch_kernel(dest_ref, xn_ref, xs_in_ref, xs_ref, sem, *, toks, words):
    del xs_in_ref

    def issue(r, c):
        src = xn_ref.at[pl.ds(pl.multiple_of(r * words, words), words)]
        for k in range(TOP_K):
            dst = dest_ref[0, 0, TOP_K * r + k]
            pltpu.make_async_copy(src, xs_ref.at[pl.ds(pl.multiple_of(dst * words, words), words)], sem).start()
        return c

    lax.fori_loop(0, toks, issue, 0, unroll=DMA_UNROLL)

    for k in range(TOP_K):
        pltpu.make_async_copy(xn_ref, xs_ref.at[pl.ds(0, toks * words)], sem).wait()


def moe_dispatch(dest, xn_lin, xs_init, toks, words):
    n = xn_lin.shape[0] // words
    return pl.pallas_call(
        functools.partial(_dispatch_kernel, toks=toks, words=words),
        out_shape=jax.ShapeDtypeStruct(xs_init.shape, xs_init.dtype),
        grid=(n // toks,),
        in_specs=[pl.BlockSpec((1, 1, TOP_K * toks), lambda i: (i, 0, 0), memory_space=pltpu.SMEM),
                  pl.BlockSpec((toks * words, LANES), lambda i: (i, 0)),
                  pl.BlockSpec(memory_space=pl.ANY)],
        out_specs=pl.BlockSpec(memory_space=pl.ANY),
        scratch_shapes=[pltpu.SemaphoreType.DMA(())],
        input_output_aliases={2: 0},
        compiler_params=_cparams(("arbitrary",)),
        name="moe_dispatch",
    )(dest, xn_lin, xs_init)


def _moe_ffn_kernel(be_ref, nu_ref, first_ref, nxt_ref, par_ref, xs_ref, wg_hbm, wu_hbm, wd_hbm, ys_ref,
                    wgs_ref, wus_ref, wds_ref, wgb_ref, wub_ref, wdb_ref, sem, *, layer, bm, words):
    i = pl.program_id(0)

    def weight_copies(e, s):
        return (pltpu.make_async_copy(wg_hbm.at[layer, e], wgs_ref.at[s], sem.at[s, 0]),
                pltpu.make_async_copy(wu_hbm.at[layer, e], wus_ref.at[s], sem.at[s, 1]),
                pltpu.make_async_copy(wd_hbm.at[layer, e], wds_ref.at[s], sem.at[s, 2]))

    @pl.when(i == 0)
    def _():
        for cp in weight_copies(be_ref[0], 0):
            cp.start()

    for s in range(2):
        @pl.when((first_ref[i] == 1) & (par_ref[i] == s))
        def _():
            for cp in weight_copies(be_ref[i], s):
                cp.wait()

            @pl.when(nxt_ref[i] >= 0)
            def _():
                for cp in weight_copies(nxt_ref[i], 1 - s):
                    cp.start()

            wgb_ref[...] = wgs_ref[s].astype(BF16)
            wub_ref[...] = wus_ref[s].astype(BF16)
            wdb_ref[...] = wds_ref[s].astype(BF16)

    @pl.when(i < nu_ref[0])
    def _():
        halves = []
        for k in range(words):
            halves.extend(_unpack_words(xs_ref[pl.ds(k, bm, stride=words), :]))
        x = jnp.concatenate(halves, axis=-1).astype(BF16)
        gt = jnp.dot(x, wgb_ref[...], preferred_element_type=F32)
        up = jnp.dot(x, wub_ref[...], preferred_element_type=F32)
        mid = (jax.nn.silu(gt) * up).astype(BF16)
        y = jnp.dot(mid, wdb_ref[...], preferred_element_type=F32)
        y = y.astype(BF16).astype(F32)
        for k in range(words):
            ys_ref[pl.ds(k, bm, stride=words), :] = _pack_rows(y, k)

    @pl.when(i >= nu_ref[0])
    def _():
        ys_ref[...] = jnp.zeros_like(ys_ref)


def moe_ffn(blk_expert, n_used, xs_lin, w_gate, w_up, w_down, layer, bm, words):
    n_blocks = xs_lin.shape[0] // (bm * words)
    _, _, d, de = w_gate.shape

    def row_map(i, be, nu, *_):
        return (jnp.minimum(i, nu[0] - 1), 0)

    ar = jnp.arange(n_blocks, dtype=I32)
    first = jnp.concatenate([jnp.ones((1,), I32), (blk_expert[1:] != blk_expert[:-1]).astype(I32)])
    parity = (jnp.cumsum(first) - 1) % 2
    start_pos = jnp.where(first == 1, ar, n_blocks)
    next_start = jnp.concatenate([lax.cummin(start_pos, reverse=True)[1:], jnp.full((1,), n_blocks, I32)])
    nxt = jnp.where(next_start < n_blocks, blk_expert[jnp.minimum(next_start, n_blocks - 1)], -1).astype(I32)

    any_spec = pl.BlockSpec(memory_space=pl.ANY)
    return pl.pallas_call(
        functools.partial(_moe_ffn_kernel, layer=layer, bm=bm, words=words),
        out_shape=jax.ShapeDtypeStruct(xs_lin.shape, U32),
        grid_spec=pltpu.PrefetchScalarGridSpec(
            num_scalar_prefetch=5,
            grid=(n_blocks,),
            in_specs=[pl.BlockSpec((bm * words, LANES), row_map), any_spec, any_spec, any_spec],
            out_specs=pl.BlockSpec((bm * words, LANES), lambda i, *_: (i, 0)),
            scratch_shapes=[pltpu.VMEM((2, d, de), F32), pltpu.VMEM((2, d, de), F32), pltpu.VMEM((2, de, d), F32),
                            pltpu.VMEM((d, de), BF16), pltpu.VMEM((d, de), BF16), pltpu.VMEM((de, d), BF16),
                            pltpu.SemaphoreType.DMA((2, 3))]),
        compiler_params=_cparams(("arbitrary",)),
        name="moe_ffn",
    )(blk_expert, n_used, first, nxt, parity.astype(I32), xs_lin, w_gate, w_up, w_down)


def _slab_pitch(words):
    return words + 2


def _combine_kernel(dc_ref, dn_ref, ys_ref, h_ref, mf_ref, g_ref, *rest, toks, words, n_tiles, nb):
    if nb:
        fo_ref, buf_ref, ub_ref, sem = rest
    else:
        ho_ref, uo_ref, buf_ref, sem = rest
    i = pl.program_id(0)
    chunks = 2 * words
    slab = _slab_pitch(words)
    slot_rows = toks * TOP_K * slab
    stride = TOP_K * slab

    def start_tile(dest_ref, slot):
        def body(r, c):
            for k in range(TOP_K):
                src = dest_ref[0, 0, TOP_K * r + k]
                pltpu.make_async_copy(ys_ref.at[pl.ds(pl.multiple_of(src * words, words), words)],
                                      buf_ref.at[pl.ds(slot * slot_rows + (r * TOP_K + k) * slab, words)],
                                      sem.at[slot]).start()
            return c

        lax.fori_loop(0, toks, body, 0, unroll=DMA_UNROLL)

    def wait_tile(slot):
        rows = toks * TOP_K * words
        pltpu.make_async_copy(ys_ref.at[pl.ds(0, rows)], buf_ref.at[pl.ds(slot * slot_rows, rows)],
                              sem.at[slot]).wait()

    def run(slot):
        if slot == 0:
            @pl.when(i == 0)
            def _():
                start_tile(dc_ref, 0)

        @pl.when(i + 1 < n_tiles)
        def _():
            start_tile(dn_ref, 1 - slot)

        wait_tile(slot)

        def expert_rows(k_slot):
            base = slot * slot_rows + k_slot * slab
            halves = []
            for k in range(words):
                halves.extend(_unpack_words(buf_ref[pl.ds(base + k, toks, stride=stride), :]))
            return jnp.concatenate(halves, axis=-1)

        mf = mf_ref[...]
        h = h_ref[...] + mf[:, 0:1] * expert_rows(0) + mf[:, 1:2] * expert_rows(1)
        if nb:
            u = _rms(h, g_ref[...])
            for c in range(chunks):
                ub_ref[c] = u[:, c * LANES:(c + 1) * LANES]
            for b in range(nb):
                fo_ref[b] = jnp.concatenate(
                    [ub_ref[c, pl.ds(b, toks // nb, stride=nb), :] for c in range(chunks)], axis=-1)
        else:
            ho_ref[...] = h
            uo_ref[...] = _rms(h, g_ref[...]).astype(uo_ref.dtype)

    @pl.when(i % 2 == 0)
    def _():
        run(0)

    @pl.when(i % 2 == 1)
    def _():
        run(1)


def moe_combine(dest, ys_lin, h, mf, g_next, u_dtype, toks, words, nb=0):
    n, d = h.shape
    n_tiles = n // toks
    row_spec = pl.BlockSpec((toks, d), lambda i: (i, 0))
    gather_buf = pltpu.VMEM((2 * toks * TOP_K * _slab_pitch(words), LANES), U32)
    if nb:
        out_shape = jax.ShapeDtypeStruct((nb, n // nb, d), F32)
        out_specs = pl.BlockSpec((nb, toks // nb, d), lambda i: (0, i, 0))
        scratch = [gather_buf, pltpu.VMEM((2 * words, toks, LANES), F32), pltpu.SemaphoreType.DMA((2,))]
    else:
        out_shape = (jax.ShapeDtypeStruct((n, d), F32), jax.ShapeDtypeStruct((n, d), u_dtype))
        out_specs = (row_spec, row_spec)
        scratch = [gather_buf, pltpu.SemaphoreType.DMA((2,))]
    return pl.pallas_call(
        functools.partial(_combine_kernel, toks=toks, words=words, n_tiles=n_tiles, nb=nb),
        out_shape=out_shape,
        grid=(n_tiles,),
        in_specs=[pl.BlockSpec((1, 1, TOP_K * toks), lambda i: (i, 0, 0), memory_space=pltpu.SMEM),
                  pl.BlockSpec((1, 1, TOP_K * toks), lambda i: (jnp.minimum(i + 1, n_tiles - 1), 0, 0),
                               memory_space=pltpu.SMEM),
                  pl.BlockSpec(memory_space=pl.ANY),
                  row_spec,
                  pl.BlockSpec((toks, LANES), lambda i: (i, 0)),
                  pl.BlockSpec((1, d), lambda i: (0, 0))],
        out_specs=out_specs,
        scratch_shapes=scratch,
        compiler_params=_cparams(("arbitrary",)),
        name="moe_combine",
    )(dest, dest, ys_lin, h, mf, g_next.reshape(1, d))


def hier_moe_layer(h, g_ffn, w_group, w_expert, w_gate, w_up, w_down, layer, g_next, u_dtype, row_buf,
                   *, rows, bm, toks_d, toks_c, final_nb=0):
    n, d = h.shape
    words = d // LANES // 2
    n_exp = w_gate.shape[1]
    xn_lin, mf, mi, cnt = router(h, g_ffn, w_group, w_expert, rows)

    counts = cnt[0, :n_exp].astype(I32)
    padded = (counts + bm - 1) // bm * bm
    pad_end = jnp.cumsum(padded)
    pad_start = pad_end - padded
    n_rows = n * TOP_K + n_exp * bm
    n_blocks = n_rows // bm
    blk_start = jnp.arange(n_blocks, dtype=I32) * bm
    blk_expert = jnp.sum((blk_start[:, None] >= pad_end[None, :]).astype(I32), axis=1)
    blk_expert = jnp.minimum(blk_expert, n_exp - 1).astype(I32)
    n_used = (pad_end[-1:] // bm).astype(I32)
    last_e = blk_expert[jnp.maximum(n_used[0] - 1, 0)]
    blk_expert = jnp.where(jnp.arange(n_blocks) < n_used[0], blk_expert, last_e)

    eid = mi[:, 0:TOP_K]
    seg = jnp.sum(jnp.where(eid[:, :, None] == jnp.arange(n_exp, dtype=I32), pad_start, 0), axis=-1)
    dest = (seg + mi[:, TOP_K:2 * TOP_K]).astype(I32)
    if row_buf is None:
        row_buf = jnp.zeros((n_rows * words, LANES), U32)
    xs_lin = moe_dispatch(dest.reshape(n // toks_d, 1, TOP_K * toks_d), xn_lin, row_buf, toks_d, words)
    ys_lin = moe_ffn(blk_expert, n_used, xs_lin, w_gate, w_up, w_down, layer, bm, words)
    dest_c = dest.reshape(n // toks_c, 1, TOP_K * toks_c)
    if final_nb:
        return None, moe_combine(dest_c, ys_lin, h, mf, g_next, u_dtype, toks_c, words, final_nb), None
    h_out, u_out = moe_combine(dest_c, ys_lin, h, mf, g_next, u_dtype, toks_c, words)
    return h_out, u_out, xs_lin


def trunk(x, norm_mix, norm_ffn, norm_final, ssm_a_re, ssm_a_im, ssm_log_dt, ssm_b_re, ssm_b_im,
          ssm_c_re, ssm_c_im, ssm_d, ssm_w_glu_a, ssm_w_glu_b, conv_w_in, conv_w, conv_w_out,
          moe_w_group_router, moe_w_expert_router, moe_w_gate, moe_w_up, moe_w_down,
          *, rows, steps, cols, cols_out, bm, toks_d, toks_c, cblock, cparts):
    bsz, seq, d = x.shape
    depth = norm_mix.shape[0]
    hh = ssm_b_re.shape[3]
    gb = cblock // hh

    h, u = norm_in(x, norm_mix[0], steps)
    row_buf = None
    for i in range(depth):
        j = i // 2
        if i % 2 == 0:
            bb_re, bb_im, l_re, l_im = ssm_prep(ssm_a_re[j], ssm_a_im[j], ssm_log_dt[j], ssm_b_re[j], ssm_b_im[j])
            wb, wc, lre, lim = _ssm_block_weights(bb_re, bb_im, ssm_c_re[j], ssm_c_im[j], l_re, l_im, gb)
            yg = s5_scan(u, wb, lre, lim, wc, ssm_d[j].astype(F32), bsz, steps, cparts)
            h = glu_residual(yg, ssm_w_glu_a[j].astype(BF16), ssm_w_glu_b[j].astype(BF16), h, rows, cols_out)
        else:
            gy = conv_in(u, conv_w_in[j].astype(BF16), conv_w[j].astype(F32), bsz, rows, cols)
            h = matmul_residual(gy, conv_w_out[j].astype(BF16), h, rows, cols_out)
        last = i == depth - 1
        g_next = norm_final if last else norm_mix[i + 1]
        u_dtype = F32 if (last or (i + 1) % 2 == 0) else BF16
        h, u, row_buf = hier_moe_layer(h, norm_ffn[i], moe_w_group_router[i], moe_w_expert_router[i],
                                       moe_w_gate, moe_w_up, moe_w_down, i, g_next, u_dtype, row_buf,
                                       rows=rows, bm=bm, toks_d=toks_d, toks_c=toks_c,
                                       final_nb=bsz if last else 0)
    return u


def kernel(x, norm_mix, norm_ffn, norm_final, ssm_a_re, ssm_a_im, ssm_log_dt, ssm_b_re, ssm_b_im, ssm_c_re, ssm_c_im, ssm_d, ssm_w_glu_a, ssm_w_glu_b, conv_w_in, conv_w, conv_w_out, moe_w_group_router, moe_w_expert_router, moe_w_gate, moe_w_up, moe_w_down):
    return trunk(x, norm_mix, norm_ffn, norm_final, ssm_a_re, ssm_a_im, ssm_log_dt, ssm_b_re, ssm_b_im,
                 ssm_c_re, ssm_c_im, ssm_d, ssm_w_glu_a, ssm_w_glu_b, conv_w_in, conv_w, conv_w_out,
                 moe_w_group_router, moe_w_expert_router, moe_w_gate, moe_w_up, moe_w_down,
                 rows=512, steps=64, cols=512, cols_out=1024, bm=256, toks_d=1024, toks_c=256, cblock=256, cparts=4)
```

```python
import functools

import jax
import jax.numpy as jnp
from jax import lax
from jax.experimental import pallas as pl
from jax.experimental.pallas import tpu as pltpu

F32 = jnp.float32
BF16 = jnp.bfloat16
I32 = jnp.int32
U32 = jnp.uint32

LANES = 128
SUBLANES = 8
VMEM_LIMIT = 56 * 1024 * 1024
RMS_EPS = 1e-6
TOP_K = 2
DMA_UNROLL = 8


def _cparams(sem):
    return pltpu.CompilerParams(dimension_semantics=sem, vmem_limit_bytes=VMEM_LIMIT)


def _rms(h, g):
    ms = jnp.mean(h * h, axis=-1, keepdims=True)
    return h * lax.rsqrt(ms + RMS_EPS) * g


def _pack_rows(x_bf16_as_f32, k):
    bits = lax.bitcast_convert_type(x_bf16_as_f32[:, (2 * k) * LANES:(2 * k + 2) * LANES], U32)
    return bits[:, :LANES] | (bits[:, LANES:] >> 16)


def _unpack_words(w):
    return (lax.bitcast_convert_type(w & jnp.uint32(0xFFFF0000), F32),
            lax.bitcast_convert_type(w << 16, F32))


def _norm_in_kernel(x_ref, g_ref, h_ref, u_ref, t_ref, *, nb, steps):
    chunks = t_ref.shape[0]
    for b in range(nb):
        xb = x_ref[b]
        for c in range(chunks):
            t_ref[c, pl.ds(b, steps, stride=nb), :] = xb[:, c * LANES:(c + 1) * LANES]
    h = jnp.concatenate([t_ref[c] for c in range(chunks)], axis=-1)
    h_ref[...] = h
    u_ref[...] = _rms(h, g_ref[...])


def norm_in(x, g, steps):
    nb, seq, d = x.shape
    rows = steps * nb
    row_spec = pl.BlockSpec((rows, d), lambda i: (i, 0))
    return pl.pallas_call(
        functools.partial(_norm_in_kernel, nb=nb, steps=steps),
        out_shape=(jax.ShapeDtypeStruct((nb * seq, d), F32), jax.ShapeDtypeStruct((nb * seq, d), F32)),
        grid=(seq // steps,),
        in_specs=[pl.BlockSpec((nb, steps, d), lambda i: (0, i, 0)),
                  pl.BlockSpec((1, d), lambda i: (0, 0))],
        out_specs=(row_spec, row_spec),
        scratch_shapes=[pltpu.VMEM((d // LANES, rows, LANES), F32)],
        compiler_params=_cparams(("parallel",)),
        name="norm_in",
    )(x, g.reshape(1, d))


def _zoh(a_re, a_im, log_dt):
    dt = jnp.exp(log_dt)
    mag = jnp.exp(a_re * dt)
    return mag * jnp.cos(a_im * dt), mag * jnp.sin(a_im * dt)


def _ssm_prep_kernel(are_ref, aim_ref, ldt_ref, bre_ref, bim_ref,
                     are2_ref, aim2_ref, ldt2_ref,
                     bbre_ref, bbim_ref, lre_ref, lim_ref):
    a_re = are_ref[...]
    a_im = aim_ref[...]
    lb_re, lb_im = _zoh(a_re, a_im, ldt_ref[...])
    den = a_re * a_re + a_im * a_im
    nr = lb_re - 1.0
    ni = lb_im
    coef_re = (nr * a_re + ni * a_im) / den
    coef_im = (ni * a_re - nr * a_im) / den
    br = bre_ref[...]
    bi = bim_ref[...]
    bbre_ref[...] = coef_re * br - coef_im * bi
    bbim_ref[...] = coef_re * bi + coef_im * br
    l_re, l_im = _zoh(are2_ref[...], aim2_ref[...], ldt2_ref[...])
    lre_ref[...] = l_re
    lim_ref[...] = l_im


def ssm_prep(a_re, a_im, log_dt, b_re, b_im):
    g, p, hh = b_re.shape
    rep = lambda a: jnp.repeat(a.astype(F32), hh, axis=1)
    ldt2 = jnp.broadcast_to(log_dt.astype(F32)[:, None], (g, p))
    outs = pl.pallas_call(
        _ssm_prep_kernel,
        out_shape=(jax.ShapeDtypeStruct((g, p * hh), F32), jax.ShapeDtypeStruct((g, p * hh), F32),
                   jax.ShapeDtypeStruct((g, p), F32), jax.ShapeDtypeStruct((g, p), F32)),
        name="ssm_prep",
    )(rep(a_re), rep(a_im), rep(ldt2), b_re.astype(F32).reshape(g, p * hh), b_im.astype(F32).reshape(g, p * hh),
      a_re.astype(F32), a_im.astype(F32), ldt2)
    bb_re, bb_im, l_re, l_im = outs
    return bb_re.reshape(g, p, hh), bb_im.reshape(g, p, hh), l_re, l_im


def _ssm_block_weights(bb_re, bb_im, c_re, c_im, l_re, l_im, gb):
    g, p, hh = bb_re.shape
    j = g // gb
    eye = jnp.eye(gb, dtype=F32)

    def emb_b(bb):
        return jnp.einsum('jgph,gk->jghkp', bb.reshape(j, gb, p, hh), eye).reshape(j, gb * hh, gb * p)

    def emb_c(c):
        return jnp.einsum('jghp,gk->jkpgh', c.reshape(j, gb, hh, p), eye).reshape(j, gb * p, gb * hh)

    wb = jnp.concatenate([emb_b(bb_re), emb_b(bb_im)], axis=2).astype(BF16)
    wc = jnp.concatenate([emb_c(c_re.astype(F32)), -emb_c(c_im.astype(F32))], axis=1).astype(BF16)
    lre = l_re.reshape(j, 1, gb * p)
    lim = l_im.reshape(j, 1, gb * p)
    return wb, wc, lre, lim


def _s5_kernel(u_ref, wb_ref, lre_ref, lim_ref, wc_ref, d_ref, o_ref,
               bu_ref, sre_ref, sim_ref, *, steps, nb, ns, cb, parts):
    @pl.when(pl.program_id(1) == 0)
    def _():
        sre_ref[...] = jnp.zeros_like(sre_ref)
        sim_ref[...] = jnp.zeros_like(sim_ref)

    def project_in(p):
        u = u_ref[:, p * cb:(p + 1) * cb]
        bu_ref[p] = jnp.dot(u.astype(BF16), wb_ref[p], preferred_element_type=F32)

    project_in(0)
    for p in range(parts):
        if p + 1 < parts:
            project_in(p + 1)
        lr = jnp.broadcast_to(lre_ref[p], (nb, ns))
        li = jnp.broadcast_to(lim_ref[p], (nb, ns))
        sr = sre_ref[p]
        si = sim_ref[p]
        for t in range(steps):
            br = bu_ref[p, t * nb:(t + 1) * nb, 0:ns]
            bi = bu_ref[p, t * nb:(t + 1) * nb, ns:2 * ns]
            sr, si = lr * sr - li * si + br, lr * si + li * sr + bi
            bu_ref[p, t * nb:(t + 1) * nb, 0:ns] = sr
            bu_ref[p, t * nb:(t + 1) * nb, ns:2 * ns] = si
        sre_ref[p] = sr
        sim_ref[p] = si
        y = jnp.dot(bu_ref[p].astype(BF16), wc_ref[p], preferred_element_type=F32)
        y = y + d_ref[:, p * cb:(p + 1) * cb] * u_ref[:, p * cb:(p + 1) * cb]
        o_ref[:, p * cb:(p + 1) * cb] = jax.nn.gelu(y).astype(o_ref.dtype)


def s5_scan(u, wb, lre, lim, wc, d_skip, nb, steps, parts):
    n, d = u.shape
    j, cb, ns2 = wb.shape
    ns = ns2 // 2
    rows = steps * nb
    jp = j // parts
    wb = wb.reshape(jp, parts, cb, ns2)
    wc = wc.reshape(jp, parts, ns2, cb)
    lre = lre.reshape(jp, parts, 1, ns)
    lim = lim.reshape(jp, parts, 1, ns)
    return pl.pallas_call(
        functools.partial(_s5_kernel, steps=steps, nb=nb, ns=ns, cb=cb, parts=parts),
        out_shape=jax.ShapeDtypeStruct((n, d), BF16),
        grid=(jp, n // rows),
        in_specs=[pl.BlockSpec((rows, parts * cb), lambda jj, i: (i, jj)),
                  pl.BlockSpec((None, parts, cb, ns2), lambda jj, i: (jj, 0, 0, 0)),
                  pl.BlockSpec((None, parts, 1, ns), lambda jj, i: (jj, 0, 0, 0)),
                  pl.BlockSpec((None, parts, 1, ns), lambda jj, i: (jj, 0, 0, 0)),
                  pl.BlockSpec((None, parts, ns2, cb), lambda jj, i: (jj, 0, 0, 0)),
                  pl.BlockSpec((1, parts * cb), lambda jj, i: (0, jj))],
        out_specs=pl.BlockSpec((rows, parts * cb), lambda jj, i: (i, jj)),
        scratch_shapes=[pltpu.VMEM((parts, rows, ns2), F32),
                        pltpu.VMEM((parts, nb, ns), F32),
                        pltpu.VMEM((parts, nb, ns), F32)],
        compiler_params=_cparams(("parallel", "arbitrary")),
        name="s5_scan",
    )(u, wb, lre, lim, wc, d_skip.reshape(1, d))


def _glu_kernel(y_ref, wa_ref, wb_ref, h_ref, o_ref):
    y = y_ref[...]
    a = jnp.dot(y, wa_ref[...], preferred_element_type=F32)
    b = jnp.dot(y, wb_ref[...], preferred_element_type=F32)
    o_ref[...] = h_ref[...] + a * jax.nn.sigmoid(b)


def glu_residual(y, wa, wb, h, rows, cols):
    n, d = y.shape
    dn = wa.shape[1]
    return pl.pallas_call(
        _glu_kernel,
        out_shape=jax.ShapeDtypeStruct((n, dn), F32),
        grid=(dn // cols, n // rows),
        in_specs=[pl.BlockSpec((rows, d), lambda c, i: (i, 0)),
                  pl.BlockSpec((d, cols), lambda c, i: (0, c)),
                  pl.BlockSpec((d, cols), lambda c, i: (0, c)),
                  pl.BlockSpec((rows, cols), lambda c, i: (i, c))],
        out_specs=pl.BlockSpec((rows, cols), lambda c, i: (i, c)),
        compiler_params=_cparams(("parallel", "parallel")),
        name="glu_residual",
    )(y, wa, wb, h)


def _mm_res_kernel(x_ref, w_ref, h_ref, o_ref):
    o_ref[...] = h_ref[...] + jnp.dot(x_ref[...], w_ref[...], preferred_element_type=F32)


def matmul_residual(x, w, h, rows, cols):
    n, d = x.shape
    dn = w.shape[1]
    return pl.pallas_call(
        _mm_res_kernel,
        out_shape=jax.ShapeDtypeStruct((n, dn), F32),
        grid=(dn // cols, n // rows),
        in_specs=[pl.BlockSpec((rows, d), lambda c, i: (i, 0)),
                  pl.BlockSpec((d, cols), lambda c, i: (0, c)),
                  pl.BlockSpec((rows, cols), lambda c, i: (i, c))],
        out_specs=pl.BlockSpec((rows, cols), lambda c, i: (i, c)),
        compiler_params=_cparams(("parallel", "parallel")),
        name="matmul_residual",
    )(x, w, h)


def _conv_in_kernel(u_ref, wbg_ref, wcg_ref, wv_ref, cw_ref, o_ref, z_ref, *, rows, nb, width):
    halo = (width - 1) * nb

    @pl.when(pl.program_id(1) == 0)
    def _():
        z_ref[0:halo, :] = jnp.zeros((halo, z_ref.shape[1]), F32)

    u = u_ref[...]
    bg = jnp.dot(u, wbg_ref[...], preferred_element_type=F32)
    cg = jnp.dot(u, wcg_ref[...], preferred_element_type=F32)
    v = jnp.dot(u, wv_ref[...], preferred_element_type=F32)
    z_ref[halo:halo + rows, :] = cg * v
    y = cw_ref[0:1, :] * z_ref[0:rows, :]
    for k in range(1, width):
        y = y + cw_ref[k:k + 1, :] * z_ref[k * nb:k * nb + rows, :]
    o_ref[...] = (bg * y).astype(o_ref.dtype)
    z_ref[0:halo, :] = z_ref[rows:rows + halo, :]


def conv_in(u, w_in, conv_w, nb, rows, cols):
    n, d = u.shape
    width = conv_w.shape[0]
    nc = d // cols
    halo = (width - 1) * nb
    return pl.pallas_call(
        functools.partial(_conv_in_kernel, rows=rows, nb=nb, width=width),
        out_shape=jax.ShapeDtypeStruct((n, d), BF16),
        grid=(nc, n // rows),
        in_specs=[pl.BlockSpec((rows, d), lambda c, i: (i, 0)),
                  pl.BlockSpec((d, cols), lambda c, i: (0, c)),
                  pl.BlockSpec((d, cols), lambda c, i: (0, c + nc)),
                  pl.BlockSpec((d, cols), lambda c, i: (0, c + 2 * nc)),
                  pl.BlockSpec((width, cols), lambda c, i: (0, c))],
        out_specs=pl.BlockSpec((rows, cols), lambda c, i: (i, c)),
        scratch_shapes=[pltpu.VMEM((rows + halo, cols), F32)],
        compiler_params=_cparams(("parallel", "arbitrary")),
        name="conv_in",
    )(u, w_in, w_in, w_in, conv_w)


def _router_kernel(h_ref, g_ref, w_ref, tri_ref, xn_ref, mf_ref, mi_ref, cnt_ref,
                   *, rows, d, n_groups, epg):
    @pl.when(pl.program_id(0) == 0)
    def _():
        cnt_ref[...] = jnp.zeros_like(cnt_ref)

    xn = _rms(h_ref[...], g_ref[...])
    x_hi = xn.astype(BF16)
    x_hi32 = x_hi.astype(F32)

    words = d // LANES // 2
    for k in range(words):
        xn_ref[pl.ds(k, rows, stride=words), :] = _pack_rows(x_hi32, k)

    x_lo = (xn - x_hi32).astype(BF16)
    both = jnp.dot(x_hi, w_ref[...], preferred_element_type=F32)
    logits = (both[:, :LANES] + both[:, LANES:]
              + jnp.dot(x_lo, w_ref[:, 0:LANES], preferred_element_type=F32))

    lane = lax.broadcasted_iota(I32, (rows, LANES), 1).astype(F32)
    neg = jnp.float32(-jnp.inf)
    big = jnp.float32(LANES)

    def first_max(vals, mask):
        m = jnp.max(jnp.where(mask, vals, neg), axis=-1, keepdims=True)
        idx = jnp.min(jnp.where(mask & (vals == m), lane, big), axis=-1, keepdims=True)
        return m, idx

    gmask = lane < n_groups
    gmax, gidx = first_max(logits, gmask)
    gsum = jnp.sum(jnp.where(gmask, jnp.exp(logits - gmax), 0.0), axis=-1, keepdims=True)
    g_w = 1.0 / gsum
    lo = n_groups + gidx * epg
    emask = (lane >= lo) & (lane < lo + epg)
    t1, i1 = first_max(logits, emask)
    t2, i2 = first_max(logits, emask & (lane != i1))
    e2 = jnp.exp(t2 - t1)
    p1 = 1.0 / (1.0 + e2)
    p2 = e2 / (1.0 + e2)
    eid1 = i1 - n_groups
    eid2 = i2 - n_groups

    onehot = ((lane == eid1) | (lane == eid2))
    oh = jnp.where(onehot, 1.0, 0.0).astype(BF16)
    before = jnp.dot(tri_ref[...], oh, preferred_element_type=F32) + cnt_ref[...]
    r1 = jnp.sum(jnp.where(lane == eid1, before, 0.0), axis=-1, keepdims=True)
    r2 = jnp.sum(jnp.where(lane == eid2, before, 0.0), axis=-1, keepdims=True)
    cnt_ref[...] = cnt_ref[...] + jnp.sum(oh.astype(F32), axis=0, keepdims=True)

    mf_ref[...] = jnp.where(lane == 0, g_w * p1, jnp.where(lane == 1, g_w * p2, 0.0))
    mi_ref[...] = jnp.where(lane == 0, eid1, jnp.where(lane == 1, eid2,
                            jnp.where(lane == 2, r1, jnp.where(lane == 3, r2, 0.0)))).astype(I32)


def router(h, g, w_group, w_expert, rows):
    n, d = h.shape
    n_groups = w_group.shape[1]
    epg = w_expert.shape[2]
    n_exp = n_groups * epg
    assert n_groups + n_exp <= LANES
    words = d // LANES // 2
    w = jnp.concatenate([w_group.astype(F32),
                         jnp.transpose(w_expert.astype(F32), (1, 0, 2)).reshape(d, n_exp)], axis=1)
    w = jnp.pad(w, ((0, 0), (0, LANES - n_groups - n_exp)))
    w_hi = w.astype(BF16)
    w_lo = (w - w_hi.astype(F32)).astype(BF16)
    w_cat = jnp.concatenate([w_hi, w_lo], axis=1)
    tri = (lax.broadcasted_iota(I32, (rows, rows), 1) < lax.broadcasted_iota(I32, (rows, rows), 0)).astype(BF16)
    xn, mf, mi, cnt = pl.pallas_call(
        functools.partial(_router_kernel, rows=rows, d=d, n_groups=n_groups, epg=epg),
        out_shape=(jax.ShapeDtypeStruct((n * words, LANES), U32),
                   jax.ShapeDtypeStruct((n, LANES), F32),
                   jax.ShapeDtypeStruct((n, LANES), I32),
                   jax.ShapeDtypeStruct((1, LANES), F32)),
        grid=(n // rows,),
        in_specs=[pl.BlockSpec((rows, d), lambda i: (i, 0)),
                  pl.BlockSpec((1, d), lambda i: (0, 0)),
                  pl.BlockSpec((d, 2 * LANES), lambda i: (0, 0)),
                  pl.BlockSpec((rows, rows), lambda i: (0, 0))],
        out_specs=(pl.BlockSpec((rows * words, LANES), lambda i: (i, 0)),
                   pl.BlockSpec((rows, LANES), lambda i: (i, 0)),
                   pl.BlockSpec((rows, LANES), lambda i: (i, 0)),
                   pl.BlockSpec((1, LANES), lambda i: (0, 0))),
        compiler_params=_cparams(("arbitrary",)),
        name="moe_router",
    )(h, g.reshape(1, d), w_cat, tri)
    return xn, mf, mi, cnt


def _dispatch_kernel(dest_ref, xn_ref, xs_in_ref, xs_ref, sem, *, toks, words):
    del xs_in_ref

    def issue(r, c):
        src = xn_ref.at[pl.ds(pl.multiple_of(r * words, words), words)]
        for k in range(TOP_K):
            dst = dest_ref[0, 0, TOP_K * r + k]
            pltpu.make_async_copy(src, xs_ref.at[pl.ds(pl.multiple_of(dst * words, words), words)], sem).start()
        return c

    lax.fori_loop(0, toks, issue, 0, unroll=DMA_UNROLL)

    for k in range(TOP_K):
        pltpu.make_async_copy(xn_ref, xs_ref.at[pl.ds(0, toks * words)], sem).wait()


def moe_dispatch(dest, xn_lin, xs_init, toks, words):
    n = xn_lin.shape[0] // words
    return pl.pallas_call(
        functools.partial(_dispatch_kernel, toks=toks, words=words),
        out_shape=jax.ShapeDtypeStruct(xs_init.shape, xs_init.dtype),
        grid=(n // toks,),
        in_specs=[pl.BlockSpec((1, 1, TOP_K * toks), lambda i: (i, 0, 0), memory_space=pltpu.SMEM),
                  pl.BlockSpec((toks * words, LANES), lambda i: (i, 0)),
                  pl.BlockSpec(memory_space=pl.ANY)],
        out_specs=pl.BlockSpec(memory_space=pl.ANY),
        scratch_shapes=[pltpu.SemaphoreType.DMA(())],
        input_output_aliases={2: 0},
        compiler_params=_cparams(("arbitrary",)),
        name="moe_dispatch",
    )(dest, xn_lin, xs_init)


def _moe_ffn_kernel(be_ref, nu_ref, first_ref, nxt_ref, par_ref, xs_ref, wg_hbm, wu_hbm, wd_hbm, ys_ref,
                    wgs_ref, wus_ref, wds_ref, wgb_ref, wub_ref, wdb_ref, sem, *, layer, bm, words):
    i = pl.program_id(0)

    def weight_copies(e, s):
        return (pltpu.make_async_copy(wg_hbm.at[layer, e], wgs_ref.at[s], sem.at[s, 0]),
                pltpu.make_async_copy(wu_hbm.at[layer, e], wus_ref.at[s], sem.at[s, 1]),
                pltpu.make_async_copy(wd_hbm.at[layer, e], wds_ref.at[s], sem.at[s, 2]))

    @pl.when(i == 0)
    def _():
        for cp in weight_copies(be_ref[0], 0):
            cp.start()

    for s in range(2):
        @pl.when((first_ref[i] == 1) & (par_ref[i] == s))
        def _():
            for cp in weight_copies(be_ref[i], s):
                cp.wait()

            @pl.when(nxt_ref[i] >= 0)
            def _():
                for cp in weight_copies(nxt_ref[i], 1 - s):
                    cp.start()

            wgb_ref[...] = wgs_ref[s].astype(BF16)
            wub_ref[...] = wus_ref[s].astype(BF16)
            wdb_ref[...] = wds_ref[s].astype(BF16)

    @pl.when(i < nu_ref[0])
    def _():
        halves = []
        for k in range(words):
            halves.extend(_unpack_words(xs_ref[pl.ds(k, bm, stride=words), :]))
        x = jnp.concatenate(halves, axis=-1).astype(BF16)
        gt = jnp.dot(x, wgb_ref[...], preferred_element_type=F32)
        up = jnp.dot(x, wub_ref[...], preferred_element_type=F32)
        mid = (jax.nn.silu(gt) * up).astype(BF16)
        y = jnp.dot(mid, wdb_ref[...], preferred_element_type=F32)
        y = y.astype(BF16).astype(F32)
        for k in range(words):
            ys_ref[pl.ds(k, bm, stride=words), :] = _pack_rows(y, k)

    @pl.when(i >= nu_ref[0])
    def _():
        ys_ref[...] = jnp.zeros_like(ys_ref)


def moe_ffn(blk_expert, n_used, xs_lin, w_gate, w_up, w_down, layer, bm, words):
    n_blocks = xs_lin.shape[0] // (bm * words)
    _, _, d, de = w_gate.shape

    def row_map(i, be, nu, *_):
        return (jnp.minimum(i, nu[0] - 1), 0)

    ar = jnp.arange(n_blocks, dtype=I32)
    first = jnp.concatenate([jnp.ones((1,), I32), (blk_expert[1:] != blk_expert[:-1]).astype(I32)])
    parity = (jnp.cumsum(first) - 1) % 2
    start_pos = jnp.where(first == 1, ar, n_blocks)
    next_start = jnp.concatenate([lax.cummin(start_pos, reverse=True)[1:], jnp.full((1,), n_blocks, I32)])
    nxt = jnp.where(next_start < n_blocks, blk_expert[jnp.minimum(next_start, n_blocks - 1)], -1).astype(I32)

    any_spec = pl.BlockSpec(memory_space=pl.ANY)
    return pl.pallas_call(
        functools.partial(_moe_ffn_kernel, layer=layer, bm=bm, words=words),
        out_shape=jax.ShapeDtypeStruct(xs_lin.shape, U32),
        grid_spec=pltpu.PrefetchScalarGridSpec(
            num_scalar_prefetch=5,
            grid=(n_blocks,),
            in_specs=[pl.BlockSpec((bm * words, LANES), row_map), any_spec, any_spec, any_spec],
            out_specs=pl.BlockSpec((bm * words, LANES), lambda i, *_: (i, 0)),
            scratch_shapes=[pltpu.VMEM((2, d, de), F32), pltpu.VMEM((2, d, de), F32), pltpu.VMEM((2, de, d), F32),
                            pltpu.VMEM((d, de), BF16), pltpu.VMEM((d, de), BF16), pltpu.VMEM((de, d), BF16),
                            pltpu.SemaphoreType.DMA((2, 3))]),
        compiler_params=_cparams(("arbitrary",)),
        name="moe_ffn",
    )(blk_expert, n_used, first, nxt, parity.astype(I32), xs_lin, w_gate, w_up, w_down)


def _slab_pitch(words):
    return words + 2


def _combine_kernel(dc_ref, dn_ref, ys_ref, h_ref, mf_ref, g_ref, *rest, toks, words, n_tiles, nb):
    if nb:
        fo_ref, buf_ref, ub_ref, sem = rest
    else:
        ho_ref, uo_ref, buf_ref, sem = rest
    i = pl.program_id(0)
    chunks = 2 * words
    slab = _slab_pitch(words)
    slot_rows = toks * TOP_K * slab
    stride = TOP_K * slab

    def start_tile(dest_ref, slot):
        def body(r, c):
            for k in range(TOP_K):
                src = dest_ref[0, 0, TOP_K * r + k]
                pltpu.make_async_copy(ys_ref.at[pl.ds(pl.multiple_of(src * words, words), words)],
                                      buf_ref.at[pl.ds(slot * slot_rows + (r * TOP_K + k) * slab, words)],
                                      sem.at[slot]).start()
            return c

        lax.fori_loop(0, toks, body, 0, unroll=DMA_UNROLL)

    def wait_tile(slot):
        rows = toks * TOP_K * words
        pltpu.make_async_copy(ys_ref.at[pl.ds(0, rows)], buf_ref.at[pl.ds(slot * slot_rows, rows)],
                              sem.at[slot]).wait()

    def run(slot):
        if slot == 0:
            @pl.when(i == 0)
            def _():
                start_tile(dc_ref, 0)

        @pl.when(i + 1 < n_tiles)
        def _():
            start_tile(dn_ref, 1 - slot)

        wait_tile(slot)

        def expert_rows(k_slot):
            base = slot * slot_rows + k_slot * slab
            halves = []
            for k in range(words):
                halves.extend(_unpack_words(buf_ref[pl.ds(base + k, toks, stride=stride), :]))
            return jnp.concatenate(halves, axis=-1)

        mf = mf_ref[...]
        h = h_ref[...] + mf[:, 0:1] * expert_rows(0) + mf[:, 1:2] * expert_rows(1)
        if nb:
            u = _rms(h, g_ref[...])
            for c in range(chunks):
                ub_ref[c] = u[:, c * LANES:(c + 1) * LANES]
            for b in range(nb):
                fo_ref[b] = jnp.concatenate(
                    [ub_ref[c, pl.ds(b, toks // nb, stride=nb), :] for c in range(chunks)], axis=-1)
        else:
            ho_ref[...] = h
            uo_ref[...] = _rms(h, g_ref[...]).astype(uo_ref.dtype)

    @pl.when(i % 2 == 0)
    def _():
        run(0)

    @pl.when(i % 2 == 1)
    def _():
        run(1)


def moe_combine(dest, ys_lin, h, mf, g_next, u_dtype, toks, words, nb=0):
    n, d = h.shape
    n_tiles = n // toks
    row_spec = pl.BlockSpec((toks, d), lambda i: (i, 0))
    gather_buf = pltpu.VMEM((2 * toks * TOP_K * _slab_pitch(words), LANES), U32)
    if nb:
        out_shape = jax.ShapeDtypeStruct((nb, n // nb, d), F32)
        out_specs = pl.BlockSpec((nb, toks // nb, d), lambda i: (0, i, 0))
        scratch = [gather_buf, pltpu.VMEM((2 * words, toks, LANES), F32), pltpu.SemaphoreType.DMA((2,))]
    else:
        out_shape = (jax.ShapeDtypeStruct((n, d), F32), jax.ShapeDtypeStruct((n, d), u_dtype))
        out_specs = (row_spec, row_spec)
        scratch = [gather_buf, pltpu.SemaphoreType.DMA((2,))]
    return pl.pallas_call(
        functools.partial(_combine_kernel, toks=toks, words=words, n_tiles=n_tiles, nb=nb),
        out_shape=out_shape,
        grid=(n_tiles,),
        in_specs=[pl.BlockSpec((1, 1, TOP_K * toks), lambda i: (i, 0, 0), memory_space=pltpu.SMEM),
                  pl.BlockSpec((1, 1, TOP_K * toks), lambda i: (jnp.minimum(i + 1, n_tiles - 1), 0, 0),
                               memory_space=pltpu.SMEM),
                  pl.BlockSpec(memory_space=pl.ANY),
                  row_spec,
                  pl.BlockSpec((toks, LANES), lambda i: (i, 0)),
                  pl.BlockSpec((1, d), lambda i: (0, 0))],
        out_specs=out_specs,
        scratch_shapes=scratch,
        compiler_params=_cparams(("arbitrary",)),
        name="moe_combine",
    )(dest, dest, ys_lin, h, mf, g_next.reshape(1, d))


def hier_moe_layer(h, g_ffn, w_group, w_expert, w_gate, w_up, w_down, layer, g_next, u_dtype, row_buf,
                   *, rows, bm, toks_d, toks_c, final_nb=0):
    n, d = h.shape
    words = d // LANES // 2
    n_exp = w_gate.shape[1]
    xn_lin, mf, mi, cnt = router(h, g_ffn, w_group, w_expert, rows)

    counts = cnt[0, :n_exp].astype(I32)
    padded = (counts + bm - 1) // bm * bm
    pad_end = jnp.cumsum(padded)
    pad_start = pad_end - padded
    n_rows = n * TOP_K + n_exp * bm
    n_blocks = n_rows // bm
    blk_start = jnp.arange(n_blocks, dtype=I32) * bm
    blk_expert = jnp.sum((blk_start[:, None] >= pad_end[None, :]).astype(I32), axis=1)
    blk_expert = jnp.minimum(blk_expert, n_exp - 1).astype(I32)
    n_used = (pad_end[-1:] // bm).astype(I32)
    last_e = blk_expert[jnp.maximum(n_used[0] - 1, 0)]
    blk_expert = jnp.where(jnp.arange(n_blocks) < n_used[0], blk_expert, last_e)

    eid = mi[:, 0:TOP_K]
    seg = jnp.sum(jnp.where(eid[:, :, None] == jnp.arange(n_exp, dtype=I32), pad_start, 0), axis=-1)
    dest = (seg + mi[:, TOP_K:2 * TOP_K]).astype(I32)
    if row_buf is None:
        row_buf = jnp.zeros((n_rows * words, LANES), U32)
    xs_lin = moe_dispatch(dest.reshape(n // toks_d, 1, TOP_K * toks_d), xn_lin, row_buf, toks_d, words)
    ys_lin = moe_ffn(blk_expert, n_used, xs_lin, w_gate, w_up, w_down, layer, bm, words)
    dest_c = dest.reshape(n // toks_c, 1, TOP_K * toks_c)
    if final_nb:
        return None, moe_combine(dest_c, ys_lin, h, mf, g_next, u_dtype, toks_c, words, final_nb), None
    h_out, u_out = moe_combine(dest_c, ys_lin, h, mf, g_next, u_dtype, toks_c, words)
    return h_out, u_out, xs_lin


def trunk(x, norm_mix, norm_ffn, norm_final, ssm_a_re, ssm_a_im, ssm_log_dt, ssm_b_re, ssm_b_im,
          ssm_c_re, ssm_c_im, ssm_d, ssm_w_glu_a, ssm_w_glu_b, conv_w_in, conv_w, conv_w_out,
          moe_w_group_router, moe_w_expert_router, moe_w_gate, moe_w_up, moe_w_down,
          *, rows, steps, cols, rows_out, cols_out, bm, toks_d, toks_c, cblock, cparts):
    bsz, seq, d = x.shape
    depth = norm_mix.shape[0]
    hh = ssm_b_re.shape[3]
    gb = cblock // hh

    h, u = norm_in(x, norm_mix[0], steps)
    row_buf = None
    for i in range(depth):
        j = i // 2
        if i % 2 == 0:
            bb_re, bb_im, l_re, l_im = ssm_prep(ssm_a_re[j], ssm_a_im[j], ssm_log_dt[j], ssm_b_re[j], ssm_b_im[j])
            wb, wc, lre, lim = _ssm_block_weights(bb_re, bb_im, ssm_c_re[j], ssm_c_im[j], l_re, l_im, gb)
            yg = s5_scan(u, wb, lre, lim, wc, ssm_d[j].astype(F32), bsz, steps, cparts)
            h = glu_residual(yg, ssm_w_glu_a[j].astype(BF16), ssm_w_glu_b[j].astype(BF16), h, rows, cols_out)
        else:
            gy = conv_in(u, conv_w_in[j].astype(BF16), conv_w[j].astype(F32), bsz, rows, cols)
            h = matmul_residual(gy, conv_w_out[j].astype(BF16), h, rows_out, cols_out)
        last = i == depth - 1
        g_next = norm_final if last else norm_mix[i + 1]
        u_dtype = F32 if (last or (i + 1) % 2 == 0) else BF16
        h, u, row_buf = hier_moe_layer(h, norm_ffn[i], moe_w_group_router[i], moe_w_expert_router[i],
                                       moe_w_gate, moe_w_up, moe_w_down, i, g_next, u_dtype, row_buf,
                                       rows=rows, bm=bm, toks_d=toks_d, toks_c=toks_c,
                                       final_nb=bsz if last else 0)
    return u


def kernel(x, norm_mix, norm_ffn, norm_final, ssm_a_re, ssm_a_im, ssm_log_dt, ssm_b_re, ssm_b_im, ssm_c_re, ssm_c_im, ssm_d, ssm_w_glu_a, ssm_w_glu_b, conv_w_in, conv_w, conv_w_out, moe_w_group_router, moe_w_expert_router, moe_w_gate, moe_w_up, moe_w_down):
    return trunk(x, norm_mix, norm_ffn, norm_final, ssm_a_re, ssm_a_im, ssm_log_dt, ssm_b_re, ssm_b_im,
                 ssm_c_re, ssm_c_im, ssm_d, ssm_w_glu_a, ssm_w_glu_b, conv_w_in, conv_w, conv_w_out,
                 moe_w_group_router, moe_w_expert_router, moe_w_gate, moe_w_up, moe_w_down,
                 rows=512, steps=64, cols=1024, rows_out=1024, cols_out=1024, bm=256, toks_d=1024, toks_c=512,
                 cblock=256, cparts=4)
```

```python
import functools

import jax
import jax.numpy as jnp
from jax import lax
from jax.experimental import pallas as pl
from jax.experimental.pallas import tpu as pltpu

F32 = jnp.float32
BF16 = jnp.bfloat16
I32 = jnp.int32
U32 = jnp.uint32

LANES = 128
SUBLANES = 8
VMEM_LIMIT = 56 * 1024 * 1024
RMS_EPS = 1e-6
TOP_K = 2
DMA_UNROLL = 8


def _cparams(sem):
    return pltpu.CompilerParams(dimension_semantics=sem, vmem_limit_bytes=VMEM_LIMIT)


def _rms(h, g):
    ms = jnp.mean(h * h, axis=-1, keepdims=True)
    return h * lax.rsqrt(ms + RMS_EPS) * g


def _pack_rows(x_bf16_as_f32, k):
    bits = lax.bitcast_convert_type(x_bf16_as_f32[:, (2 * k) * LANES:(2 * k + 2) * LANES], U32)
    return bits[:, :LANES] | (bits[:, LANES:] >> 16)


def _unpack_words(w):
    return (lax.bitcast_convert_type(w & jnp.uint32(0xFFFF0000), F32),
            lax.bitcast_convert_type(w << 16, F32))


def _norm_in_kernel(x_ref, g_ref, h_ref, u_ref, t_ref, *, nb, steps):
    chunks = t_ref.shape[0]
    for b in range(nb):
        xb = x_ref[b]
        for c in range(chunks):
            t_ref[c, pl.ds(b, steps, stride=nb), :] = xb[:, c * LANES:(c + 1) * LANES]
    h = jnp.concatenate([t_ref[c] for c in range(chunks)], axis=-1)
    h_ref[...] = h
    u_ref[...] = _rms(h, g_ref[...])


def norm_in(x, g, steps):
    nb, seq, d = x.shape
    rows = steps * nb
    row_spec = pl.BlockSpec((rows, d), lambda i: (i, 0))
    return pl.pallas_call(
        functools.partial(_norm_in_kernel, nb=nb, steps=steps),
        out_shape=(jax.ShapeDtypeStruct((nb * seq, d), F32), jax.ShapeDtypeStruct((nb * seq, d), F32)),
        grid=(seq // steps,),
        in_specs=[pl.BlockSpec((nb, steps, d), lambda i: (0, i, 0)),
                  pl.BlockSpec((1, d), lambda i: (0, 0))],
        out_specs=(row_spec, row_spec),
        scratch_shapes=[pltpu.VMEM((d // LANES, rows, LANES), F32)],
        compiler_params=_cparams(("parallel",)),
        name="norm_in",
    )(x, g.reshape(1, d))


def _zoh(a_re, a_im, log_dt):
    dt = jnp.exp(log_dt)
    mag = jnp.exp(a_re * dt)
    return mag * jnp.cos(a_im * dt), mag * jnp.sin(a_im * dt)


def _ssm_prep_kernel(are_ref, aim_ref, ldt_ref, bre_ref, bim_ref,
                     are2_ref, aim2_ref, ldt2_ref,
                     bbre_ref, bbim_ref, lre_ref, lim_ref):
    a_re = are_ref[...]
    a_im = aim_ref[...]
    lb_re, lb_im = _zoh(a_re, a_im, ldt_ref[...])
    den = a_re * a_re + a_im * a_im
    nr = lb_re - 1.0
    ni = lb_im
    coef_re = (nr * a_re + ni * a_im) / den
    coef_im = (ni * a_re - nr * a_im) / den
    br = bre_ref[...]
    bi = bim_ref[...]
    bbre_ref[...] = coef_re * br - coef_im * bi
    bbim_ref[...] = coef_re * bi + coef_im * br
    l_re, l_im = _zoh(are2_ref[...], aim2_ref[...], ldt2_ref[...])
    lre_ref[...] = l_re
    lim_ref[...] = l_im


def ssm_prep(a_re, a_im, log_dt, b_re, b_im):
    g, p, hh = b_re.shape
    rep = lambda a: jnp.repeat(a.astype(F32), hh, axis=1)
    ldt2 = jnp.broadcast_to(log_dt.astype(F32)[:, None], (g, p))
    outs = pl.pallas_call(
        _ssm_prep_kernel,
        out_shape=(jax.ShapeDtypeStruct((g, p * hh), F32), jax.ShapeDtypeStruct((g, p * hh), F32),
                   jax.ShapeDtypeStruct((g, p), F32), jax.ShapeDtypeStruct((g, p), F32)),
        name="ssm_prep",
    )(rep(a_re), rep(a_im), rep(ldt2), b_re.astype(F32).reshape(g, p * hh), b_im.astype(F32).reshape(g, p * hh),
      a_re.astype(F32), a_im.astype(F32), ldt2)
    bb_re, bb_im, l_re, l_im = outs
    return bb_re.reshape(g, p, hh), bb_im.reshape(g, p, hh), l_re, l_im


def _ssm_block_weights(bb_re, bb_im, c_re, c_im, l_re, l_im, gb):
    g, p, hh = bb_re.shape
    j = g // gb
    ch_group = jnp.arange(gb * hh, dtype=I32) // hh
    st_group = jnp.arange(gb * p, dtype=I32) // p

    def emb_b(bb):
        t = jnp.transpose(bb.reshape(j, gb, p, hh), (0, 1, 3, 2)).reshape(j, gb * hh, p)
        return jnp.where(ch_group[:, None] == st_group[None, :], jnp.tile(t, (1, 1, gb)), 0.0)

    def emb_c(c):
        t = jnp.transpose(c.reshape(j, gb, hh, p), (0, 3, 1, 2)).reshape(j, p, gb * hh)
        return jnp.where(st_group[:, None] == ch_group[None, :], jnp.tile(t, (1, gb, 1)), 0.0)

    wb = jnp.concatenate([emb_b(bb_re), emb_b(bb_im)], axis=2).astype(BF16)
    wc = jnp.concatenate([emb_c(c_re.astype(F32)), -emb_c(c_im.astype(F32))], axis=1).astype(BF16)
    lre = l_re.reshape(j, 1, gb * p)
    lim = l_im.reshape(j, 1, gb * p)
    return wb, wc, lre, lim


def _s5_kernel(u_ref, wb_ref, lre_ref, lim_ref, wc_ref, d_ref, o_ref,
               bu_ref, sre_ref, sim_ref, *, steps, nb, ns, cb, parts):
    @pl.when(pl.program_id(1) == 0)
    def _():
        sre_ref[...] = jnp.zeros_like(sre_ref)
        sim_ref[...] = jnp.zeros_like(sim_ref)

    def project_in(p):
        u = u_ref[:, p * cb:(p + 1) * cb]
        bu_ref[p] = jnp.dot(u.astype(BF16), wb_ref[p], preferred_element_type=F32)

    project_in(0)
    for p in range(parts):
        if p + 1 < parts:
            project_in(p + 1)
        lr = jnp.broadcast_to(lre_ref[p], (nb, ns))
        li = jnp.broadcast_to(lim_ref[p], (nb, ns))
        sr = sre_ref[p]
        si = sim_ref[p]
        for t in range(steps):
            br = bu_ref[p, t * nb:(t + 1) * nb, 0:ns]
            bi = bu_ref[p, t * nb:(t + 1) * nb, ns:2 * ns]
            sr, si = lr * sr - li * si + br, lr * si + li * sr + bi
            bu_ref[p, t * nb:(t + 1) * nb, 0:ns] = sr
            bu_ref[p, t * nb:(t + 1) * nb, ns:2 * ns] = si
        sre_ref[p] = sr
        sim_ref[p] = si
        y = jnp.dot(bu_ref[p].astype(BF16), wc_ref[p], preferred_element_type=F32)
        y = y + d_ref[:, p * cb:(p + 1) * cb] * u_ref[:, p * cb:(p + 1) * cb]
        o_ref[:, p * cb:(p + 1) * cb] = jax.nn.gelu(y).astype(o_ref.dtype)


def s5_scan(u, wb, lre, lim, wc, d_skip, nb, steps, parts):
    n, d = u.shape
    j, cb, ns2 = wb.shape
    ns = ns2 // 2
    rows = steps * nb
    jp = j // parts
    wb = wb.reshape(jp, parts, cb, ns2)
    wc = wc.reshape(jp, parts, ns2, cb)
    lre = lre.reshape(jp, parts, 1, ns)
    lim = lim.reshape(jp, parts, 1, ns)
    return pl.pallas_call(
        functools.partial(_s5_kernel, steps=steps, nb=nb, ns=ns, cb=cb, parts=parts),
        out_shape=jax.ShapeDtypeStruct((n, d), BF16),
        grid=(jp, n // rows),
        in_specs=[pl.BlockSpec((rows, parts * cb), lambda jj, i: (i, jj)),
                  pl.BlockSpec((None, parts, cb, ns2), lambda jj, i: (jj, 0, 0, 0)),
                  pl.BlockSpec((None, parts, 1, ns), lambda jj, i: (jj, 0, 0, 0)),
                  pl.BlockSpec((None, parts, 1, ns), lambda jj, i: (jj, 0, 0, 0)),
                  pl.BlockSpec((None, parts, ns2, cb), lambda jj, i: (jj, 0, 0, 0)),
                  pl.BlockSpec((1, parts * cb), lambda jj, i: (0, jj))],
        out_specs=pl.BlockSpec((rows, parts * cb), lambda jj, i: (i, jj)),
        scratch_shapes=[pltpu.VMEM((parts, rows, ns2), F32),
                        pltpu.VMEM((parts, nb, ns), F32),
                        pltpu.VMEM((parts, nb, ns), F32)],
        compiler_params=_cparams(("parallel", "arbitrary")),
        name="s5_scan",
    )(u, wb, lre, lim, wc, d_skip.reshape(1, d))


def _glu_kernel(y_ref, wa_ref, wb_ref, h_ref, o_ref):
    y = y_ref[...]
    a = jnp.dot(y, wa_ref[...], preferred_element_type=F32)
    b = jnp.dot(y, wb_ref[...], preferred_element_type=F32)
    o_ref[...] = h_ref[...] + a * jax.nn.sigmoid(b)


def glu_residual(y, wa, wb, h, rows, cols):
    n, d = y.shape
    dn = wa.shape[1]
    return pl.pallas_call(
        _glu_kernel,
        out_shape=jax.ShapeDtypeStruct((n, dn), F32),
        grid=(dn // cols, n // rows),
        in_specs=[pl.BlockSpec((rows, d), lambda c, i: (i, 0)),
                  pl.BlockSpec((d, cols), lambda c, i: (0, c)),
                  pl.BlockSpec((d, cols), lambda c, i: (0, c)),
                  pl.BlockSpec((rows, cols), lambda c, i: (i, c))],
        out_specs=pl.BlockSpec((rows, cols), lambda c, i: (i, c)),
        compiler_params=_cparams(("parallel", "parallel")),
        name="glu_residual",
    )(y, wa, wb, h)


def _mm_res_kernel(x_ref, w_ref, h_ref, o_ref):
    o_ref[...] = h_ref[...] + jnp.dot(x_ref[...], w_ref[...], preferred_element_type=F32)


def matmul_residual(x, w, h, rows, cols):
    n, d = x.shape
    dn = w.shape[1]
    return pl.pallas_call(
        _mm_res_kernel,
        out_shape=jax.ShapeDtypeStruct((n, dn), F32),
        grid=(dn // cols, n // rows),
        in_specs=[pl.BlockSpec((rows, d), lambda c, i: (i, 0)),
                  pl.BlockSpec((d, cols), lambda c, i: (0, c)),
                  pl.BlockSpec((rows, cols), lambda c, i: (i, c))],
        out_specs=pl.BlockSpec((rows, cols), lambda c, i: (i, c)),
        compiler_params=_cparams(("parallel", "parallel")),
        name="matmul_residual",
    )(x, w, h)


def _conv_in_kernel(u_ref, wbg_ref, wcg_ref, wv_ref, cw_ref, o_ref, z_ref, *, rows, nb, width):
    halo = (width - 1) * nb

    @pl.when(pl.program_id(1) == 0)
    def _():
        z_ref[0:halo, :] = jnp.zeros((halo, z_ref.shape[1]), F32)

    u = u_ref[...]
    bg = jnp.dot(u, wbg_ref[...], preferred_element_type=F32)
    cg = jnp.dot(u, wcg_ref[...], preferred_element_type=F32)
    v = jnp.dot(u, wv_ref[...], preferred_element_type=F32)
    z_ref[halo:halo + rows, :] = cg * v
    y = cw_ref[0:1, :] * z_ref[0:rows, :]
    for k in range(1, width):
        y = y + cw_ref[k:k + 1, :] * z_ref[k * nb:k * nb + rows, :]
    o_ref[...] = (bg * y).astype(o_ref.dtype)
    z_ref[0:halo, :] = z_ref[rows:rows + halo, :]


def conv_in(u, w_in, conv_w, nb, rows, cols):
    n, d = u.shape
    width = conv_w.shape[0]
    nc = d // cols
    halo = (width - 1) * nb
    return pl.pallas_call(
        functools.partial(_conv_in_kernel, rows=rows, nb=nb, width=width),
        out_shape=jax.ShapeDtypeStruct((n, d), BF16),
        grid=(nc, n // rows),
        in_specs=[pl.BlockSpec((rows, d), lambda c, i: (i, 0)),
                  pl.BlockSpec((d, cols), lambda c, i: (0, c)),
                  pl.BlockSpec((d, cols), lambda c, i: (0, c + nc)),
                  pl.BlockSpec((d, cols), lambda c, i: (0, c + 2 * nc)),
                  pl.BlockSpec((width, cols), lambda c, i: (0, c))],
        out_specs=pl.BlockSpec((rows, cols), lambda c, i: (i, c)),
        scratch_shapes=[pltpu.VMEM((rows + halo, cols), F32)],
        compiler_params=_cparams(("parallel", "arbitrary")),
        name="conv_in",
    )(u, w_in, w_in, w_in, conv_w)


def _router_kernel(h_ref, g_ref, w_ref, tri_ref, xn_ref, mf_ref, mi_ref, cnt_ref,
                   *, rows, d, n_groups, epg):
    @pl.when(pl.program_id(0) == 0)
    def _():
        cnt_ref[...] = jnp.zeros_like(cnt_ref)

    xn = _rms(h_ref[...], g_ref[...])
    x_hi = xn.astype(BF16)
    x_hi32 = x_hi.astype(F32)

    words = d // LANES // 2
    for k in range(words):
        xn_ref[pl.ds(k, rows, stride=words), :] = _pack_rows(x_hi32, k)

    x_lo = (xn - x_hi32).astype(BF16)
    both = jnp.dot(x_hi, w_ref[...], preferred_element_type=F32)
    logits = (both[:, :LANES] + both[:, LANES:]
              + jnp.dot(x_lo, w_ref[:, 0:LANES], preferred_element_type=F32))

    lane = lax.broadcasted_iota(I32, (rows, LANES), 1).astype(F32)
    neg = jnp.float32(-jnp.inf)
    big = jnp.float32(LANES)

    def first_max(vals, mask):
        m = jnp.max(jnp.where(mask, vals, neg), axis=-1, keepdims=True)
        idx = jnp.min(jnp.where(mask & (vals == m), lane, big), axis=-1, keepdims=True)
        return m, idx

    gmask = lane < n_groups
    gmax, gidx = first_max(logits, gmask)
    gsum = jnp.sum(jnp.where(gmask, jnp.exp(logits - gmax), 0.0), axis=-1, keepdims=True)
    g_w = 1.0 / gsum
    lo = n_groups + gidx * epg
    emask = (lane >= lo) & (lane < lo + epg)
    t1, i1 = first_max(logits, emask)
    t2, i2 = first_max(logits, emask & (lane != i1))
    e2 = jnp.exp(t2 - t1)
    p1 = 1.0 / (1.0 + e2)
    p2 = e2 / (1.0 + e2)
    eid1 = i1 - n_groups
    eid2 = i2 - n_groups

    onehot = ((lane == eid1) | (lane == eid2))
    oh = jnp.where(onehot, 1.0, 0.0).astype(BF16)
    before = jnp.dot(tri_ref[...], oh, preferred_element_type=F32) + cnt_ref[...]
    r1 = jnp.sum(jnp.where(lane == eid1, before, 0.0), axis=-1, keepdims=True)
    r2 = jnp.sum(jnp.where(lane == eid2, before, 0.0), axis=-1, keepdims=True)
    cnt_ref[...] = cnt_ref[...] + jnp.sum(oh.astype(F32), axis=0, keepdims=True)

    mf_ref[...] = jnp.where(lane == 0, g_w * p1, jnp.where(lane == 1, g_w * p2, 0.0))
    mi_ref[...] = jnp.where(lane == 0, eid1, jnp.where(lane == 1, eid2,
                            jnp.where(lane == 2, r1, jnp.where(lane == 3, r2, 0.0)))).astype(I32)


def router(h, g, w_group, w_expert, rows):
    n, d = h.shape
    n_groups = w_group.shape[1]
    epg = w_expert.shape[2]
    n_exp = n_groups * epg
    assert n_groups + n_exp <= LANES
    words = d // LANES // 2
    w = jnp.concatenate([w_group.astype(F32),
                         jnp.transpose(w_expert.astype(F32), (1, 0, 2)).reshape(d, n_exp)], axis=1)
    w = jnp.pad(w, ((0, 0), (0, LANES - n_groups - n_exp)))
    w_hi = w.astype(BF16)
    w_lo = (w - w_hi.astype(F32)).astype(BF16)
    w_cat = jnp.concatenate([w_hi, w_lo], axis=1)
    tri = (lax.broadcasted_iota(I32, (rows, rows), 1) < lax.broadcasted_iota(I32, (rows, rows), 0)).astype(BF16)
    xn, mf, mi, cnt = pl.pallas_call(
        functools.partial(_router_kernel, rows=rows, d=d, n_groups=n_groups, epg=epg),
        out_shape=(jax.ShapeDtypeStruct((n * words, LANES), U32),
                   jax.ShapeDtypeStruct((n, LANES), F32),
                   jax.ShapeDtypeStruct((n, LANES), I32),
                   jax.ShapeDtypeStruct((1, LANES), F32)),
        grid=(n // rows,),
        in_specs=[pl.BlockSpec((rows, d), lambda i: (i, 0)),
                  pl.BlockSpec((1, d), lambda i: (0, 0)),
                  pl.BlockSpec((d, 2 * LANES), lambda i: (0, 0)),
                  pl.BlockSpec((rows, rows), lambda i: (0, 0))],
        out_specs=(pl.BlockSpec((rows * words, LANES), lambda i: (i, 0)),
                   pl.BlockSpec((rows, LANES), lambda i: (i, 0)),
                   pl.BlockSpec((rows, LANES), lambda i: (i, 0)),
                   pl.BlockSpec((1, LANES), lambda i: (0, 0))),
        compiler_params=_cparams(("arbitrary",)),
        name="moe_router",
    )(h, g.reshape(1, d), w_cat, tri)
    return xn, mf, mi, cnt


def _dispatch_kernel(dest_ref, xn_ref, xs_in_ref, xs_ref, sem, *, toks, words):
    del xs_in_ref

    def issue(r, c):
        src = xn_ref.at[pl.ds(pl.multiple_of(r * words, words), words)]
        for k in range(TOP_K):
            dst = dest_ref[0, 0, TOP_K * r + k]
            pltpu.make_async_copy(src, xs_ref.at[pl.ds(pl.multiple_of(dst * words, words), words)], sem).start()
        return c

    lax.fori_loop(0, toks, issue, 0, unroll=DMA_UNROLL)

    for k in range(TOP_K):
        pltpu.make_async_copy(xn_ref, xs_ref.at[pl.ds(0, toks * words)], sem).wait()


def moe_dispatch(dest, xn_lin, xs_init, toks, words):
    n = xn_lin.shape[0] // words
    return pl.pallas_call(
        functools.partial(_dispatch_kernel, toks=toks, words=words),
        out_shape=jax.ShapeDtypeStruct(xs_init.shape, xs_init.dtype),
        grid=(n // toks,),
        in_specs=[pl.BlockSpec((1, 1, TOP_K * toks), lambda i: (i, 0, 0), memory_space=pltpu.SMEM),
                  pl.BlockSpec((toks * words, LANES), lambda i: (i, 0)),
                  pl.BlockSpec(memory_space=pl.ANY)],
        out_specs=pl.BlockSpec(memory_space=pl.ANY),
        scratch_shapes=[pltpu.SemaphoreType.DMA(())],
        input_output_aliases={2: 0},
        compiler_params=_cparams(("arbitrary",)),
        name="moe_dispatch",
    )(dest, xn_lin, xs_init)


def _moe_ffn_kernel(be_ref, nu_ref, first_ref, nxt_ref, par_ref, xs_ref, wg_hbm, wu_hbm, wd_hbm, ys_ref,
                    wgs_ref, wus_ref, wds_ref, wgb_ref, wub_ref, wdb_ref, sem, *, layer, bm, words):
    i = pl.program_id(0)

    def weight_copies(e, s):
        return (pltpu.make_async_copy(wg_hbm.at[layer, e], wgs_ref.at[s], sem.at[s, 0]),
                pltpu.make_async_copy(wu_hbm.at[layer, e], wus_ref.at[s], sem.at[s, 1]),
                pltpu.make_async_copy(wd_hbm.at[layer, e], wds_ref.at[s], sem.at[s, 2]))

    @pl.when(i == 0)
    def _():
        for cp in weight_copies(be_ref[0], 0):
            cp.start()

    for s in range(2):
        @pl.when((first_ref[i] == 1) & (par_ref[i] == s))
        def _():
            for cp in weight_copies(be_ref[i], s):
                cp.wait()

            @pl.when(nxt_ref[i] >= 0)
            def _():
                for cp in weight_copies(nxt_ref[i], 1 - s):
                    cp.start()

            wgb_ref[...] = wgs_ref[s].astype(BF16)
            wub_ref[...] = wus_ref[s].astype(BF16)
            wdb_ref[...] = wds_ref[s].astype(BF16)

    @pl.when(i < nu_ref[0])
    def _():
        halves = []
        for k in range(words):
            halves.extend(_unpack_words(xs_ref[pl.ds(k, bm, stride=words), :]))
        x = jnp.concatenate(halves, axis=-1).astype(BF16)
        gt = jnp.dot(x, wgb_ref[...], preferred_element_type=F32)
        up = jnp.dot(x, wub_ref[...], preferred_element_type=F32)
        mid = (jax.nn.silu(gt) * up).astype(BF16)
        y = jnp.dot(mid, wdb_ref[...], preferred_element_type=F32)
        y = y.astype(BF16).astype(F32)
        for k in range(words):
            ys_ref[pl.ds(k, bm, stride=words), :] = _pack_rows(y, k)

    @pl.when(i >= nu_ref[0])
    def _():
        ys_ref[...] = jnp.zeros_like(ys_ref)


def moe_ffn(blk_expert, n_used, xs_lin, w_gate, w_up, w_down, layer, bm, words):
    n_blocks = xs_lin.shape[0] // (bm * words)
    _, _, d, de = w_gate.shape

    def row_map(i, be, nu, *_):
        return (jnp.minimum(i, nu[0] - 1), 0)

    ar = jnp.arange(n_blocks, dtype=I32)
    first = jnp.concatenate([jnp.ones((1,), I32), (blk_expert[1:] != blk_expert[:-1]).astype(I32)])
    parity = (jnp.cumsum(first) - 1) % 2
    start_pos = jnp.where(first == 1, ar, n_blocks)
    next_start = jnp.concatenate([lax.cummin(start_pos, reverse=True)[1:], jnp.full((1,), n_blocks, I32)])
    nxt = jnp.where(next_start < n_blocks, blk_expert[jnp.minimum(next_start, n_blocks - 1)], -1).astype(I32)

    any_spec = pl.BlockSpec(memory_space=pl.ANY)
    return pl.pallas_call(
        functools.partial(_moe_ffn_kernel, layer=layer, bm=bm, words=words),
        out_shape=jax.ShapeDtypeStruct(xs_lin.shape, U32),
        grid_spec=pltpu.PrefetchScalarGridSpec(
            num_scalar_prefetch=5,
            grid=(n_blocks,),
            in_specs=[pl.BlockSpec((bm * words, LANES), row_map), any_spec, any_spec, any_spec],
            out_specs=pl.BlockSpec((bm * words, LANES), lambda i, *_: (i, 0)),
            scratch_shapes=[pltpu.VMEM((2, d, de), F32), pltpu.VMEM((2, d, de), F32), pltpu.VMEM((2, de, d), F32),
                            pltpu.VMEM((d, de), BF16), pltpu.VMEM((d, de), BF16), pltpu.VMEM((de, d), BF16),
                            pltpu.SemaphoreType.DMA((2, 3))]),
        compiler_params=_cparams(("arbitrary",)),
        name="moe_ffn",
    )(blk_expert, n_used, first, nxt, parity.astype(I32), xs_lin, w_gate, w_up, w_down)


def _slab_pitch(words):
    return words + 2


def _combine_kernel(dc_ref, dn_ref, ys_ref, h_ref, mf_ref, g_ref, *rest, toks, words, n_tiles, nb):
    if nb:
        fo_ref, buf_ref, ub_ref, sem = rest
    else:
        ho_ref, uo_ref, buf_ref, sem = rest
    i = pl.program_id(0)
    chunks = 2 * words
    slab = _slab_pitch(words)
    slot_rows = toks * TOP_K * slab
    stride = TOP_K * slab

    def start_tile(dest_ref, slot):
        def body(r, c):
            for k in range(TOP_K):
                src = dest_ref[0, 0, TOP_K * r + k]
                pltpu.make_async_copy(ys_ref.at[pl.ds(pl.multiple_of(src * words, words), words)],
                                      buf_ref.at[pl.ds(slot * slot_rows + (r * TOP_K + k) * slab, words)],
                                      sem.at[slot]).start()
            return c

        lax.fori_loop(0, toks, body, 0, unroll=DMA_UNROLL)

    def wait_tile(slot):
        rows = toks * TOP_K * words
        pltpu.make_async_copy(ys_ref.at[pl.ds(0, rows)], buf_ref.at[pl.ds(slot * slot_rows, rows)],
                              sem.at[slot]).wait()

    def run(slot):
        if slot == 0:
            @pl.when(i == 0)
            def _():
                start_tile(dc_ref, 0)

        @pl.when(i + 1 < n_tiles)
        def _():
            start_tile(dn_ref, 1 - slot)

        wait_tile(slot)

        def expert_rows(k_slot):
            base = slot * slot_rows + k_slot * slab
            halves = []
            for k in range(words):
                halves.extend(_unpack_words(buf_ref[pl.ds(base + k, toks, stride=stride), :]))
            return jnp.concatenate(halves, axis=-1)

        mf = mf_ref[...]
        h = h_ref[...] + mf[:, 0:1] * expert_rows(0) + mf[:, 1:2] * expert_rows(1)
        if nb:
            u = _rms(h, g_ref[...])
            for c in range(chunks):
                ub_ref[c] = u[:, c * LANES:(c + 1) * LANES]
            for b in range(nb):
                fo_ref[b] = jnp.concatenate(
                    [ub_ref[c, pl.ds(b, toks // nb, stride=nb), :] for c in range(chunks)], axis=-1)
        else:
            ho_ref[...] = h
            uo_ref[...] = _rms(h, g_ref[...]).astype(uo_ref.dtype)

    @pl.when(i % 2 == 0)
    def _():
        run(0)

    @pl.when(i % 2 == 1)
    def _():
        run(1)


def moe_combine(dest, ys_lin, h, mf, g_next, u_dtype, toks, words, nb=0):
    n, d = h.shape
    n_tiles = n // toks
    row_spec = pl.BlockSpec((toks, d), lambda i: (i, 0))
    gather_buf = pltpu.VMEM((2 * toks * TOP_K * _slab_pitch(words), LANES), U32)
    if nb:
        out_shape = jax.ShapeDtypeStruct((nb, n // nb, d), F32)
        out_specs = pl.BlockSpec((nb, toks // nb, d), lambda i: (0, i, 0))
        scratch = [gather_buf, pltpu.VMEM((2 * words, toks, LANES), F32), pltpu.SemaphoreType.DMA((2,))]
    else:
        out_shape = (jax.ShapeDtypeStruct((n, d), F32), jax.ShapeDtypeStruct((n, d), u_dtype))
        out_specs = (row_spec, row_spec)
        scratch = [gather_buf, pltpu.SemaphoreType.DMA((2,))]
    return pl.pallas_call(
        functools.partial(_combine_kernel, toks=toks, words=words, n_tiles=n_tiles, nb=nb),
        out_shape=out_shape,
        grid=(n_tiles,),
        in_specs=[pl.BlockSpec((1, 1, TOP_K * toks), lambda i: (i, 0, 0), memory_space=pltpu.SMEM),
                  pl.BlockSpec((1, 1, TOP_K * toks), lambda i: (jnp.minimum(i + 1, n_tiles - 1), 0, 0),
                               memory_space=pltpu.SMEM),
                  pl.BlockSpec(memory_space=pl.ANY),
                  row_spec,
                  pl.BlockSpec((toks, LANES), lambda i: (i, 0)),
                  pl.BlockSpec((1, d), lambda i: (0, 0))],
        out_specs=out_specs,
        scratch_shapes=scratch,
        compiler_params=_cparams(("arbitrary",)),
        name="moe_combine",
    )(dest, dest, ys_lin, h, mf, g_next.reshape(1, d))


def hier_moe_layer(h, g_ffn, w_group, w_expert, w_gate, w_up, w_down, layer, g_next, u_dtype, row_buf,
                   *, rows, bm, toks_d, toks_c, final_nb=0):
    n, d = h.shape
    words = d // LANES // 2
    n_exp = w_gate.shape[1]
    xn_lin, mf, mi, cnt = router(h, g_ffn, w_group, w_expert, rows)

    counts = cnt[0, :n_exp].astype(I32)
    padded = (counts + bm - 1) // bm * bm
    pad_end = jnp.cumsum(padded)
    pad_start = pad_end - padded
    n_rows = n * TOP_K + n_exp * bm
    n_blocks = n_rows // bm
    blk_start = jnp.arange(n_blocks, dtype=I32) * bm
    blk_expert = jnp.sum((blk_start[:, None] >= pad_end[None, :]).astype(I32), axis=1)
    blk_expert = jnp.minimum(blk_expert, n_exp - 1).astype(I32)
    n_used = (pad_end[-1:] // bm).astype(I32)
    last_e = blk_expert[jnp.maximum(n_used[0] - 1, 0)]
    blk_expert = jnp.where(jnp.arange(n_blocks) < n_used[0], blk_expert, last_e)

    eid = mi[:, 0:TOP_K]
    seg = jnp.sum(jnp.where(eid[:, :, None] == jnp.arange(n_exp, dtype=I32), pad_start, 0), axis=-1)
    dest = (seg + mi[:, TOP_K:2 * TOP_K]).astype(I32)
    if row_buf is None:
        row_buf = jnp.zeros((n_rows * words, LANES), U32)
    xs_lin = moe_dispatch(dest.reshape(n // toks_d, 1, TOP_K * toks_d), xn_lin, row_buf, toks_d, words)
    ys_lin = moe_ffn(blk_expert, n_used, xs_lin, w_gate, w_up, w_down, layer, bm, words)
    dest_c = dest.reshape(n // toks_c, 1, TOP_K * toks_c)
    if final_nb:
        return None, moe_combine(dest_c, ys_lin, h, mf, g_next, u_dtype, toks_c, words, final_nb), None
    h_out, u_out = moe_combine(dest_c, ys_lin, h, mf, g_next, u_dtype, toks_c, words)
    return h_out, u_out, xs_lin


def trunk(x, norm_mix, norm_ffn, norm_final, ssm_a_re, ssm_a_im, ssm_log_dt, ssm_b_re, ssm_b_im,
          ssm_c_re, ssm_c_im, ssm_d, ssm_w_glu_a, ssm_w_glu_b, conv_w_in, conv_w, conv_w_out,
          moe_w_group_router, moe_w_expert_router, moe_w_gate, moe_w_up, moe_w_down,
          *, rows, steps, cols, rows_out, cols_out, bm, toks_d, toks_c, cblock, cparts):
    bsz, seq, d = x.shape
    depth = norm_mix.shape[0]
    hh = ssm_b_re.shape[3]
    gb = cblock // hh

    h, u = norm_in(x, norm_mix[0], steps)
    row_buf = None
    for i in range(depth):
        j = i // 2
        if i % 2 == 0:
            bb_re, bb_im, l_re, l_im = ssm_prep(ssm_a_re[j], ssm_a_im[j], ssm_log_dt[j], ssm_b_re[j], ssm_b_im[j])
            wb, wc, lre, lim = _ssm_block_weights(bb_re, bb_im, ssm_c_re[j], ssm_c_im[j], l_re, l_im, gb)
            yg = s5_scan(u, wb, lre, lim, wc, ssm_d[j].astype(F32), bsz, steps, cparts)
            h = glu_residual(yg, ssm_w_glu_a[j].astype(BF16), ssm_w_glu_b[j].astype(BF16), h, rows_out, cols_out)
        else:
            gy = conv_in(u, conv_w_in[j].astype(BF16), conv_w[j].astype(F32), bsz, rows, cols)
            h = matmul_residual(gy, conv_w_out[j].astype(BF16), h, rows_out, cols_out)
        last = i == depth - 1
        g_next = norm_final if last else norm_mix[i + 1]
        u_dtype = F32 if (last or (i + 1) % 2 == 0) else BF16
        h, u, row_buf = hier_moe_layer(h, norm_ffn[i], moe_w_group_router[i], moe_w_expert_router[i],
                                       moe_w_gate, moe_w_up, moe_w_down, i, g_next, u_dtype, row_buf,
                                       rows=rows, bm=bm, toks_d=toks_d, toks_c=toks_c,
                                       final_nb=bsz if last else 0)
    return u


def kernel(x, norm_mix, norm_ffn, norm_final, ssm_a_re, ssm_a_im, ssm_log_dt, ssm_b_re, ssm_b_im, ssm_c_re, ssm_c_im, ssm_d, ssm_w_glu_a, ssm_w_glu_b, conv_w_in, conv_w, conv_w_out, moe_w_group_router, moe_w_expert_router, moe_w_gate, moe_w_up, moe_w_down):
    return trunk(x, norm_mix, norm_ffn, norm_final, ssm_a_re, ssm_a_im, ssm_log_dt, ssm_b_re, ssm_b_im,
                 ssm_c_re, ssm_c_im, ssm_d, ssm_w_glu_a, ssm_w_glu_b, conv_w_in, conv_w, conv_w_out,
                 moe_w_group_router, moe_w_expert_router, moe_w_gate, moe_w_up, moe_w_down,
                 rows=512, steps=64, cols=1024, rows_out=1024, cols_out=1024, bm=256, toks_d=1024, toks_c=256,
                 cblock=256, cparts=4)
```

```python
import functools

import jax
import jax.numpy as jnp
from jax import lax
from jax.experimental import pallas as pl
from jax.experimental.pallas import tpu as pltpu

F32 = jnp.float32
BF16 = jnp.bfloat16
I32 = jnp.int32
U32 = jnp.uint32

LANES = 128
SUBLANES = 8
VMEM_LIMIT = 56 * 1024 * 1024
RMS_EPS = 1e-6
TOP_K = 2
DMA_UNROLL = 8


def _cparams(sem):
    return pltpu.CompilerParams(dimension_semantics=sem, vmem_limit_bytes=VMEM_LIMIT)


def _rms(h, g):
    ms = jnp.mean(h * h, axis=-1, keepdims=True)
    return h * lax.rsqrt(ms + RMS_EPS) * g


def _pack_rows(x_bf16_as_f32, k):
    bits = lax.bitcast_convert_type(x_bf16_as_f32[:, (2 * k) * LANES:(2 * k + 2) * LANES], U32)
    return bits[:, :LANES] | (bits[:, LANES:] >> 16)


def _unpack_words(w):
    return (lax.bitcast_convert_type(w & jnp.uint32(0xFFFF0000), F32),
            lax.bitcast_convert_type(w << 16, F32))


def _norm_in_kernel(x_ref, g_ref, h_ref, u_ref, t_ref, *, nb, steps):
    chunks = t_ref.shape[0]
    for b in range(nb):
        xb = x_ref[b]
        for c in range(chunks):
            t_ref[c, pl.ds(b, steps, stride=nb), :] = xb[:, c * LANES:(c + 1) * LANES]
    h = jnp.concatenate([t_ref[c] for c in range(chunks)], axis=-1)
    h_ref[...] = h
    u_ref[...] = _rms(h, g_ref[...])


def norm_in(x, g, steps):
    nb, seq, d = x.shape
    rows = steps * nb
    row_spec = pl.BlockSpec((rows, d), lambda i: (i, 0))
    return pl.pallas_call(
        functools.partial(_norm_in_kernel, nb=nb, steps=steps),
        out_shape=(jax.ShapeDtypeStruct((nb * seq, d), F32), jax.ShapeDtypeStruct((nb * seq, d), F32)),
        grid=(seq // steps,),
        in_specs=[pl.BlockSpec((nb, steps, d), lambda i: (0, i, 0)),
                  pl.BlockSpec((1, d), lambda i: (0, 0))],
        out_specs=(row_spec, row_spec),
        scratch_shapes=[pltpu.VMEM((d // LANES, rows, LANES), F32)],
        compiler_params=_cparams(("parallel",)),
        name="norm_in",
    )(x, g.reshape(1, d))


def _zoh(a_re, a_im, log_dt):
    dt = jnp.exp(log_dt)
    mag = jnp.exp(a_re * dt)
    return mag * jnp.cos(a_im * dt), mag * jnp.sin(a_im * dt)


def _ssm_prep_kernel(are_ref, aim_ref, ldt_ref, bre_ref, bim_ref,
                     are2_ref, aim2_ref, ldt2_ref,
                     bbre_ref, bbim_ref, lre_ref, lim_ref):
    a_re = are_ref[...]
    a_im = aim_ref[...]
    lb_re, lb_im = _zoh(a_re, a_im, ldt_ref[...])
    den = a_re * a_re + a_im * a_im
    nr = lb_re - 1.0
    ni = lb_im
    coef_re = (nr * a_re + ni * a_im) / den
    coef_im = (ni * a_re - nr * a_im) / den
    br = bre_ref[...]
    bi = bim_ref[...]
    bbre_ref[...] = coef_re * br - coef_im * bi
    bbim_ref[...] = coef_re * bi + coef_im * br
    l_re, l_im = _zoh(are2_ref[...], aim2_ref[...], ldt2_ref[...])
    lre_ref[...] = l_re
    lim_ref[...] = l_im


def ssm_prep(a_re, a_im, log_dt, b_re, b_im):
    g, p, hh = b_re.shape
    rep = lambda a: jnp.repeat(a.astype(F32), hh, axis=1)
    ldt2 = jnp.broadcast_to(log_dt.astype(F32)[:, None], (g, p))
    outs = pl.pallas_call(
        _ssm_prep_kernel,
        out_shape=(jax.ShapeDtypeStruct((g, p * hh), F32), jax.ShapeDtypeStruct((g, p * hh), F32),
                   jax.ShapeDtypeStruct((g, p), F32), jax.ShapeDtypeStruct((g, p), F32)),
        name="ssm_prep",
    )(rep(a_re), rep(a_im), rep(ldt2), b_re.astype(F32).reshape(g, p * hh), b_im.astype(F32).reshape(g, p * hh),
      a_re.astype(F32), a_im.astype(F32), ldt2)
    bb_re, bb_im, l_re, l_im = outs
    return bb_re.reshape(g, p, hh), bb_im.reshape(g, p, hh), l_re, l_im


def _ssm_block_weights(bb_re, bb_im, c_re, c_im, l_re, l_im, gb):
    g, p, hh = bb_re.shape
    j = g // gb
    ch_group = jnp.arange(gb * hh, dtype=I32) // hh
    st_group = jnp.arange(gb * p, dtype=I32) // p

    def emb_b(bb):
        t = jnp.transpose(bb.reshape(j, gb, p, hh), (0, 1, 3, 2)).reshape(j, gb * hh, p)
        return jnp.where(ch_group[:, None] == st_group[None, :], jnp.tile(t, (1, 1, gb)), 0.0)

    def emb_c(c):
        t = jnp.transpose(c.reshape(j, gb, hh, p), (0, 3, 1, 2)).reshape(j, p, gb * hh)
        return jnp.where(st_group[:, None] == ch_group[None, :], jnp.tile(t, (1, gb, 1)), 0.0)

    wb = jnp.concatenate([emb_b(bb_re), emb_b(bb_im)], axis=2).astype(BF16)
    wc = jnp.concatenate([emb_c(c_re.astype(F32)), -emb_c(c_im.astype(F32))], axis=1).astype(BF16)
    lre = l_re.reshape(j, 1, gb * p)
    lim = l_im.reshape(j, 1, gb * p)
    return wb, wc, lre, lim


def _s5_kernel(u_ref, wb_ref, lre_ref, lim_ref, wc_ref, d_ref, o_ref,
               bu_ref, sre_ref, sim_ref, *, steps, nb, ns, cb, parts):
    @pl.when(pl.program_id(1) == 0)
    def _():
        sre_ref[...] = jnp.zeros_like(sre_ref)
        sim_ref[...] = jnp.zeros_like(sim_ref)

    def project_in(p):
        u = u_ref[:, p * cb:(p + 1) * cb]
        bu_ref[p] = jnp.dot(u.astype(BF16), wb_ref[p], preferred_element_type=F32)

    project_in(0)
    for p in range(parts):
        if p + 1 < parts:
            project_in(p + 1)
        lr = jnp.broadcast_to(lre_ref[p], (nb, ns))
        li = jnp.broadcast_to(lim_ref[p], (nb, ns))
        sr = sre_ref[p]
        si = sim_ref[p]
        for t in range(steps):
            br = bu_ref[p, t * nb:(t + 1) * nb, 0:ns]
            bi = bu_ref[p, t * nb:(t + 1) * nb, ns:2 * ns]
            sr, si = lr * sr - li * si + br, lr * si + li * sr + bi
            bu_ref[p, t * nb:(t + 1) * nb, 0:ns] = sr
            bu_ref[p, t * nb:(t + 1) * nb, ns:2 * ns] = si
        sre_ref[p] = sr
        sim_ref[p] = si
        y = jnp.dot(bu_ref[p].astype(BF16), wc_ref[p], preferred_element_type=F32)
        y = y + d_ref[:, p * cb:(p + 1) * cb] * u_ref[:, p * cb:(p + 1) * cb]
        o_ref[:, p * cb:(p + 1) * cb] = jax.nn.gelu(y).astype(o_ref.dtype)


def s5_scan(u, wb, lre, lim, wc, d_skip, nb, steps, parts):
    n, d = u.shape
    j, cb, ns2 = wb.shape
    ns = ns2 // 2
    rows = steps * nb
    jp = j // parts
    wb = wb.reshape(jp, parts, cb, ns2)
    wc = wc.reshape(jp, parts, ns2, cb)
    lre = lre.reshape(jp, parts, 1, ns)
    lim = lim.reshape(jp, parts, 1, ns)
    return pl.pallas_call(
        functools.partial(_s5_kernel, steps=steps, nb=nb, ns=ns, cb=cb, parts=parts),
        out_shape=jax.ShapeDtypeStruct((n, d), BF16),
        grid=(jp, n // rows),
        in_specs=[pl.BlockSpec((rows, parts * cb), lambda jj, i: (i, jj)),
                  pl.BlockSpec((None, parts, cb, ns2), lambda jj, i: (jj, 0, 0, 0)),
                  pl.BlockSpec((None, parts, 1, ns), lambda jj, i: (jj, 0, 0, 0)),
                  pl.BlockSpec((None, parts, 1, ns), lambda jj, i: (jj, 0, 0, 0)),
                  pl.BlockSpec((None, parts, ns2, cb), lambda jj, i: (jj, 0, 0, 0)),
                  pl.BlockSpec((1, parts * cb), lambda jj, i: (0, jj))],
        out_specs=pl.BlockSpec((rows, parts * cb), lambda jj, i: (i, jj)),
        scratch_shapes=[pltpu.VMEM((parts, rows, ns2), F32),
                        pltpu.VMEM((parts, nb, ns), F32),
                        pltpu.VMEM((parts, nb, ns), F32)],
        compiler_params=_cparams(("parallel", "arbitrary")),
        name="s5_scan",
    )(u, wb, lre, lim, wc, d_skip.reshape(1, d))


def _glu_kernel(y_ref, wa_ref, wb_ref, h_ref, o_ref):
    y = y_ref[...]
    a = jnp.dot(y, wa_ref[...], preferred_element_type=F32)
    b = jnp.dot(y, wb_ref[...], preferred_element_type=F32)
    o_ref[...] = h_ref[...] + a * jax.nn.sigmoid(b)


def glu_residual(y, wa, wb, h, rows, cols):
    n, d = y.shape
    dn = wa.shape[1]
    return pl.pallas_call(
        _glu_kernel,
        out_shape=jax.ShapeDtypeStruct((n, dn), F32),
        grid=(dn // cols, n // rows),
        in_specs=[pl.BlockSpec((rows, d), lambda c, i: (i, 0)),
                  pl.BlockSpec((d, cols), lambda c, i: (0, c)),
                  pl.BlockSpec((d, cols), lambda c, i: (0, c)),
                  pl.BlockSpec((rows, cols), lambda c, i: (i, c))],
        out_specs=pl.BlockSpec((rows, cols), lambda c, i: (i, c)),
        compiler_params=_cparams(("parallel", "parallel")),
        name="glu_residual",
    )(y, wa, wb, h)


def _mm_res_kernel(x_ref, w_ref, h_ref, o_ref):
    o_ref[...] = h_ref[...] + jnp.dot(x_ref[...], w_ref[...], preferred_element_type=F32)


def matmul_residual(x, w, h, rows, cols):
    n, d = x.shape
    dn = w.shape[1]
    return pl.pallas_call(
        _mm_res_kernel,
        out_shape=jax.ShapeDtypeStruct((n, dn), F32),
        grid=(dn // cols, n // rows),
        in_specs=[pl.BlockSpec((rows, d), lambda c, i: (i, 0)),
                  pl.BlockSpec((d, cols), lambda c, i: (0, c)),
                  pl.BlockSpec((rows, cols), lambda c, i: (i, c))],
        out_specs=pl.BlockSpec((rows, cols), lambda c, i: (i, c)),
        compiler_params=_cparams(("parallel", "parallel")),
        name="matmul_residual",
    )(x, w, h)


def _conv_in_kernel(u_ref, wbg_ref, wcg_ref, wv_ref, cw_ref, o_ref, z_ref, *, rows, nb, width):
    halo = (width - 1) * nb

    @pl.when(pl.program_id(1) == 0)
    def _():
        z_ref[0:halo, :] = jnp.zeros((halo, z_ref.shape[1]), F32)

    u = u_ref[...]
    bg = jnp.dot(u, wbg_ref[...], preferred_element_type=F32)
    cg = jnp.dot(u, wcg_ref[...], preferred_element_type=F32)
    v = jnp.dot(u, wv_ref[...], preferred_element_type=F32)
    z_ref[halo:halo + rows, :] = cg * v
    y = cw_ref[0:1, :] * z_ref[0:rows, :]
    for k in range(1, width):
        y = y + cw_ref[k:k + 1, :] * z_ref[k * nb:k * nb + rows, :]
    o_ref[...] = (bg * y).astype(o_ref.dtype)
    z_ref[0:halo, :] = z_ref[rows:rows + halo, :]


def conv_in(u, w_in, conv_w, nb, rows, cols):
    n, d = u.shape
    width = conv_w.shape[0]
    nc = d // cols
    halo = (width - 1) * nb
    return pl.pallas_call(
        functools.partial(_conv_in_kernel, rows=rows, nb=nb, width=width),
        out_shape=jax.ShapeDtypeStruct((n, d), BF16),
        grid=(nc, n // rows),
        in_specs=[pl.BlockSpec((rows, d), lambda c, i: (i, 0)),
                  pl.BlockSpec((d, cols), lambda c, i: (0, c)),
                  pl.BlockSpec((d, cols), lambda c, i: (0, c + nc)),
                  pl.BlockSpec((d, cols), lambda c, i: (0, c + 2 * nc)),
                  pl.BlockSpec((width, cols), lambda c, i: (0, c))],
        out_specs=pl.BlockSpec((rows, cols), lambda c, i: (i, c)),
        scratch_shapes=[pltpu.VMEM((rows + halo, cols), F32)],
        compiler_params=_cparams(("parallel", "arbitrary")),
        name="conv_in",
    )(u, w_in, w_in, w_in, conv_w)


def _router_kernel(h_ref, g_ref, w_ref, tri_ref, xn_ref, mf_ref, mi_ref, cnt_ref,
                   *, rows, d, n_groups, epg):
    @pl.when(pl.program_id(0) == 0)
    def _():
        cnt_ref[...] = jnp.zeros_like(cnt_ref)

    xn = _rms(h_ref[...], g_ref[...])
    x_hi = xn.astype(BF16)
    x_hi32 = x_hi.astype(F32)

    words = d // LANES // 2
    for k in range(words):
        xn_ref[pl.ds(k, rows, stride=words), :] = _pack_rows(x_hi32, k)

    x_lo = (xn - x_hi32).astype(BF16)
    both = jnp.dot(x_hi, w_ref[...], preferred_element_type=F32)
    logits = (both[:, :LANES] + both[:, LANES:]
              + jnp.dot(x_lo, w_ref[:, 0:LANES], preferred_element_type=F32))

    lane = lax.broadcasted_iota(I32, (rows, LANES), 1).astype(F32)
    neg = jnp.float32(-jnp.inf)
    big = jnp.float32(LANES)

    def first_max(vals, mask):
        m = jnp.max(jnp.where(mask, vals, neg), axis=-1, keepdims=True)
        idx = jnp.min(jnp.where(mask & (vals == m), lane, big), axis=-1, keepdims=True)
        return m, idx

    gmask = lane < n_groups
    gmax, gidx = first_max(logits, gmask)
    gsum = jnp.sum(jnp.where(gmask, jnp.exp(logits - gmax), 0.0), axis=-1, keepdims=True)
    g_w = 1.0 / gsum
    lo = n_groups + gidx * epg
    emask = (lane >= lo) & (lane < lo + epg)
    t1, i1 = first_max(logits, emask)
    t2, i2 = first_max(logits, emask & (lane != i1))
    e2 = jnp.exp(t2 - t1)
    p1 = 1.0 / (1.0 + e2)
    p2 = e2 / (1.0 + e2)
    eid1 = i1 - n_groups
    eid2 = i2 - n_groups

    onehot = ((lane == eid1) | (lane == eid2))
    oh = jnp.where(onehot, 1.0, 0.0).astype(BF16)
    before = jnp.dot(tri_ref[...], oh, preferred_element_type=F32) + cnt_ref[...]
    r1 = jnp.sum(jnp.where(lane == eid1, before, 0.0), axis=-1, keepdims=True)
    r2 = jnp.sum(jnp.where(lane == eid2, before, 0.0), axis=-1, keepdims=True)
    cnt_ref[...] = cnt_ref[...] + jnp.sum(oh.astype(F32), axis=0, keepdims=True)

    mf_ref[...] = jnp.where(lane == 0, g_w * p1, jnp.where(lane == 1, g_w * p2, 0.0))
    mi_ref[...] = jnp.where(lane == 0, eid1, jnp.where(lane == 1, eid2,
                            jnp.where(lane == 2, r1, jnp.where(lane == 3, r2, 0.0)))).astype(I32)


def router(h, g, w_group, w_expert, rows):
    n, d = h.shape
    n_groups = w_group.shape[1]
    epg = w_expert.shape[2]
    n_exp = n_groups * epg
    assert n_groups + n_exp <= LANES
    words = d // LANES // 2
    w = jnp.concatenate([w_group.astype(F32),
                         jnp.transpose(w_expert.astype(F32), (1, 0, 2)).reshape(d, n_exp)], axis=1)
    w = jnp.pad(w, ((0, 0), (0, LANES - n_groups - n_exp)))
    w_hi = w.astype(BF16)
    w_lo = (w - w_hi.astype(F32)).astype(BF16)
    w_cat = jnp.concatenate([w_hi, w_lo], axis=1)
    tri = (lax.broadcasted_iota(I32, (rows, rows), 1) < lax.broadcasted_iota(I32, (rows, rows), 0)).astype(BF16)
    xn, mf, mi, cnt = pl.pallas_call(
        functools.partial(_router_kernel, rows=rows, d=d, n_groups=n_groups, epg=epg),
        out_shape=(jax.ShapeDtypeStruct((n * words, LANES), U32),
                   jax.ShapeDtypeStruct((n, LANES), F32),
                   jax.ShapeDtypeStruct((n, LANES), I32),
                   jax.ShapeDtypeStruct((1, LANES), F32)),
        grid=(n // rows,),
        in_specs=[pl.BlockSpec((rows, d), lambda i: (i, 0)),
                  pl.BlockSpec((1, d), lambda i: (0, 0)),
                  pl.BlockSpec((d, 2 * LANES), lambda i: (0, 0)),
                  pl.BlockSpec((rows, rows), lambda i: (0, 0))],
        out_specs=(pl.BlockSpec((rows * words, LANES), lambda i: (i, 0)),
                   pl.BlockSpec((rows, LANES), lambda i: (i, 0)),
                   pl.BlockSpec((rows, LANES), lambda i: (i, 0)),
                   pl.BlockSpec((1, LANES), lambda i: (0, 0))),
        compiler_params=_cparams(("arbitrary",)),
        name="moe_router",
    )(h, g.reshape(1, d), w_cat, tri)
    return xn, mf, mi, cnt


def _dispatch_kernel(dest_ref, xn_ref, xs_in_ref, xs_ref, sem, *, toks, words):
    del xs_in_ref

    def issue(r, c):
        src = xn_ref.at[pl.ds(pl.multiple_of(r * words, words), words)]
        for k in range(TOP_K):
            dst = dest_ref[0, 0, TOP_K * r + k]
            pltpu.make_async_copy(src, xs_ref.at[pl.ds(pl.multiple_of(dst * words, words), words)],
                                  sem).start(priority=k % 2)
        return c

    lax.fori_loop(0, toks, issue, 0, unroll=DMA_UNROLL)

    for k in range(TOP_K):
        pltpu.make_async_copy(xn_ref, xs_ref.at[pl.ds(0, toks * words)], sem).wait()


def moe_dispatch(dest, xn_lin, xs_init, toks, words):
    n = xn_lin.shape[0] // words
    return pl.pallas_call(
        functools.partial(_dispatch_kernel, toks=toks, words=words),
        out_shape=jax.ShapeDtypeStruct(xs_init.shape, xs_init.dtype),
        grid=(n // toks,),
        in_specs=[pl.BlockSpec((1, 1, TOP_K * toks), lambda i: (i, 0, 0), memory_space=pltpu.SMEM),
                  pl.BlockSpec((toks * words, LANES), lambda i: (i, 0)),
                  pl.BlockSpec(memory_space=pl.ANY)],
        out_specs=pl.BlockSpec(memory_space=pl.ANY),
        scratch_shapes=[pltpu.SemaphoreType.DMA(())],
        input_output_aliases={2: 0},
        compiler_params=_cparams(("arbitrary",)),
        name="moe_dispatch",
    )(dest, xn_lin, xs_init)


def _moe_ffn_kernel(be_ref, nu_ref, first_ref, nxt_ref, par_ref, xs_ref, wg_hbm, wu_hbm, wd_hbm, ys_ref,
                    wgs_ref, wus_ref, wds_ref, wgb_ref, wub_ref, wdb_ref, sem, *, layer, bm, words):
    i = pl.program_id(0)

    def weight_copies(e, s):
        return (pltpu.make_async_copy(wg_hbm.at[layer, e], wgs_ref.at[s], sem.at[s, 0]),
                pltpu.make_async_copy(wu_hbm.at[layer, e], wus_ref.at[s], sem.at[s, 1]),
                pltpu.make_async_copy(wd_hbm.at[layer, e], wds_ref.at[s], sem.at[s, 2]))

    @pl.when(i == 0)
    def _():
        for cp in weight_copies(be_ref[0], 0):
            cp.start()

    for s in range(2):
        @pl.when((first_ref[i] == 1) & (par_ref[i] == s))
        def _():
            for cp in weight_copies(be_ref[i], s):
                cp.wait()

            @pl.when(nxt_ref[i] >= 0)
            def _():
                for cp in weight_copies(nxt_ref[i], 1 - s):
                    cp.start()

            wgb_ref[...] = wgs_ref[s].astype(BF16)
            wub_ref[...] = wus_ref[s].astype(BF16)
            wdb_ref[...] = wds_ref[s].astype(BF16)

    @pl.when(i < nu_ref[0])
    def _():
        halves = []
        for k in range(words):
            halves.extend(_unpack_words(xs_ref[pl.ds(k, bm, stride=words), :]))
        x = jnp.concatenate(halves, axis=-1).astype(BF16)
        gt = jnp.dot(x, wgb_ref[...], preferred_element_type=F32)
        up = jnp.dot(x, wub_ref[...], preferred_element_type=F32)
        mid = (jax.nn.silu(gt) * up).astype(BF16)
        y = jnp.dot(mid, wdb_ref[...], preferred_element_type=F32)
        y = y.astype(BF16).astype(F32)
        for k in range(words):
            ys_ref[pl.ds(k, bm, stride=words), :] = _pack_rows(y, k)

    @pl.when(i >= nu_ref[0])
    def _():
        ys_ref[...] = jnp.zeros_like(ys_ref)


def moe_ffn(blk_expert, n_used, xs_lin, w_gate, w_up, w_down, layer, bm, words):
    n_blocks = xs_lin.shape[0] // (bm * words)
    _, _, d, de = w_gate.shape

    def row_map(i, be, nu, *_):
        return (jnp.minimum(i, nu[0] - 1), 0)

    ar = jnp.arange(n_blocks, dtype=I32)
    first = jnp.concatenate([jnp.ones((1,), I32), (blk_expert[1:] != blk_expert[:-1]).astype(I32)])
    parity = (jnp.cumsum(first) - 1) % 2
    start_pos = jnp.where(first == 1, ar, n_blocks)
    next_start = jnp.concatenate([lax.cummin(start_pos, reverse=True)[1:], jnp.full((1,), n_blocks, I32)])
    nxt = jnp.where(next_start < n_blocks, blk_expert[jnp.minimum(next_start, n_blocks - 1)], -1).astype(I32)

    any_spec = pl.BlockSpec(memory_space=pl.ANY)
    return pl.pallas_call(
        functools.partial(_moe_ffn_kernel, layer=layer, bm=bm, words=words),
        out_shape=jax.ShapeDtypeStruct(xs_lin.shape, U32),
        grid_spec=pltpu.PrefetchScalarGridSpec(
            num_scalar_prefetch=5,
            grid=(n_blocks,),
            in_specs=[pl.BlockSpec((bm * words, LANES), row_map), any_spec, any_spec, any_spec],
            out_specs=pl.BlockSpec((bm * words, LANES), lambda i, *_: (i, 0)),
            scratch_shapes=[pltpu.VMEM((2, d, de), F32), pltpu.VMEM((2, d, de), F32), pltpu.VMEM((2, de, d), F32),
                            pltpu.VMEM((d, de), BF16), pltpu.VMEM((d, de), BF16), pltpu.VMEM((de, d), BF16),
                            pltpu.SemaphoreType.DMA((2, 3))]),
        compiler_params=_cparams(("arbitrary",)),
        name="moe_ffn",
    )(blk_expert, n_used, first, nxt, parity.astype(I32), xs_lin, w_gate, w_up, w_down)


def _slab_pitch(words):
    return words + 2


def _combine_kernel(dc_ref, dn_ref, ys_ref, h_ref, mf_ref, g_ref, *rest, toks, words, n_tiles, nb):
    if nb:
        fo_ref, buf_ref, ub_ref, sem = rest
    else:
        ho_ref, uo_ref, buf_ref, sem = rest
    i = pl.program_id(0)
    chunks = 2 * words
    slab = _slab_pitch(words)
    slot_rows = toks * TOP_K * slab
    stride = TOP_K * slab

    def start_tile(dest_ref, slot):
        def body(r, c):
            for k in range(TOP_K):
                src = dest_ref[0, 0, TOP_K * r + k]
                pltpu.make_async_copy(ys_ref.at[pl.ds(pl.multiple_of(src * words, words), words)],
                                      buf_ref.at[pl.ds(slot * slot_rows + (r * TOP_K + k) * slab, words)],
                                      sem.at[slot]).start(priority=k % 2)
            return c

        lax.fori_loop(0, toks, body, 0, unroll=DMA_UNROLL)

    def wait_tile(slot):
        rows = toks * TOP_K * words
        pltpu.make_async_copy(ys_ref.at[pl.ds(0, rows)], buf_ref.at[pl.ds(slot * slot_rows, rows)],
                              sem.at[slot]).wait()

    def run(slot):
        if slot == 0:
            @pl.when(i == 0)
            def _():
                start_tile(dc_ref, 0)

        @pl.when(i + 1 < n_tiles)
        def _():
            start_tile(dn_ref, 1 - slot)

        wait_tile(slot)

        def expert_rows(k_slot):
            base = slot * slot_rows + k_slot * slab
            halves = []
            for k in range(words):
                halves.extend(_unpack_words(buf_ref[pl.ds(base + k, toks, stride=stride), :]))
            return jnp.concatenate(halves, axis=-1)

        mf = mf_ref[...]
        h = h_ref[...] + mf[:, 0:1] * expert_rows(0) + mf[:, 1:2] * expert_rows(1)
        if nb:
            u = _rms(h, g_ref[...])
            for c in range(chunks):
                ub_ref[c] = u[:, c * LANES:(c + 1) * LANES]
            for b in range(nb):
                fo_ref[b] = jnp.concatenate(
                    [ub_ref[c, pl.ds(b, toks // nb, stride=nb), :] for c in range(chunks)], axis=-1)
        else:
            ho_ref[...] = h
            uo_ref[...] = _rms(h, g_ref[...]).astype(uo_ref.dtype)

    @pl.when(i % 2 == 0)
    def _():
        run(0)

    @pl.when(i % 2 == 1)
    def _():
        run(1)


def moe_combine(dest, ys_lin, h, mf, g_next, u_dtype, toks, words, nb=0):
    n, d = h.shape
    n_tiles = n // toks
    row_spec = pl.BlockSpec((toks, d), lambda i: (i, 0))
    gather_buf = pltpu.VMEM((2 * toks * TOP_K * _slab_pitch(words), LANES), U32)
    if nb:
        out_shape = jax.ShapeDtypeStruct((nb, n // nb, d), F32)
        out_specs = pl.BlockSpec((nb, toks // nb, d), lambda i: (0, i, 0))
        scratch = [gather_buf, pltpu.VMEM((2 * words, toks, LANES), F32), pltpu.SemaphoreType.DMA((2,))]
    else:
        out_shape = (jax.ShapeDtypeStruct((n, d), F32), jax.ShapeDtypeStruct((n, d), u_dtype))
        out_specs = (row_spec, row_spec)
        scratch = [gather_buf, pltpu.SemaphoreType.DMA((2,))]
    return pl.pallas_call(
        functools.partial(_combine_kernel, toks=toks, words=words, n_tiles=n_tiles, nb=nb),
        out_shape=out_shape,
        grid=(n_tiles,),
        in_specs=[pl.BlockSpec((1, 1, TOP_K * toks), lambda i: (i, 0, 0), memory_space=pltpu.SMEM),
                  pl.BlockSpec((1, 1, TOP_K * toks), lambda i: (jnp.minimum(i + 1, n_tiles - 1), 0, 0),
                               memory_space=pltpu.SMEM),
                  pl.BlockSpec(memory_space=pl.ANY),
                  row_spec,
                  pl.BlockSpec((toks, LANES), lambda i: (i, 0)),
                  pl.BlockSpec((1, d), lambda i: (0, 0))],
        out_specs=out_specs,
        scratch_shapes=scratch,
        compiler_params=_cparams(("arbitrary",)),
        name="moe_combine",
    )(dest, dest, ys_lin, h, mf, g_next.reshape(1, d))


def hier_moe_layer(h, g_ffn, w_group, w_expert, w_gate, w_up, w_down, layer, g_next, u_dtype, row_buf,
                   *, rows, bm, toks_d, toks_c, final_nb=0):
    n, d = h.shape
    words = d // LANES // 2
    n_exp = w_gate.shape[1]
    xn_lin, mf, mi, cnt = router(h, g_ffn, w_group, w_expert, rows)

    counts = cnt[0, :n_exp].astype(I32)
    padded = (counts + bm - 1) // bm * bm
    pad_end = jnp.cumsum(padded)
    pad_start = pad_end - padded
    n_rows = n * TOP_K + n_exp * bm
    n_blocks = n_rows // bm
    blk_start = jnp.arange(n_blocks, dtype=I32) * bm
    blk_expert = jnp.sum((blk_start[:, None] >= pad_end[None, :]).astype(I32), axis=1)
    blk_expert = jnp.minimum(blk_expert, n_exp - 1).astype(I32)
    n_used = (pad_end[-1:] // bm).astype(I32)
    last_e = blk_expert[jnp.maximum(n_used[0] - 1, 0)]
    blk_expert = jnp.where(jnp.arange(n_blocks) < n_used[0], blk_expert, last_e)

    eid = mi[:, 0:TOP_K]
    seg = jnp.sum(jnp.where(eid[:, :, None] == jnp.arange(n_exp, dtype=I32), pad_start, 0), axis=-1)
    dest = (seg + mi[:, TOP_K:2 * TOP_K]).astype(I32)
    if row_buf is None:
        row_buf = jnp.zeros((n_rows * words, LANES), U32)
    xs_lin = moe_dispatch(dest.reshape(n // toks_d, 1, TOP_K * toks_d), xn_lin, row_buf, toks_d, words)
    ys_lin = moe_ffn(blk_expert, n_used, xs_lin, w_gate, w_up, w_down, layer, bm, words)
    dest_c = dest.reshape(n // toks_c, 1, TOP_K * toks_c)
    if final_nb:
        return None, moe_combine(dest_c, ys_lin, h, mf, g_next, u_dtype, toks_c, words, final_nb), None
    h_out, u_out = moe_combine(dest_c, ys_lin, h, mf, g_next, u_dtype, toks_c, words)
    return h_out, u_out, xs_lin


def trunk(x, norm_mix, norm_ffn, norm_final, ssm_a_re, ssm_a_im, ssm_log_dt, ssm_b_re, ssm_b_im,
          ssm_c_re, ssm_c_im, ssm_d, ssm_w_glu_a, ssm_w_glu_b, conv_w_in, conv_w, conv_w_out,
          moe_w_group_router, moe_w_expert_router, moe_w_gate, moe_w_up, moe_w_down,
          *, rows, steps, cols, rows_out, cols_out, bm, toks_d, toks_c, cblock, cparts):
    bsz, seq, d = x.shape
    depth = norm_mix.shape[0]
    hh = ssm_b_re.shape[3]
    gb = cblock // hh

    h, u = norm_in(x, norm_mix[0], steps)
    row_buf = None
    for i in range(depth):
        j = i // 2
        if i % 2 == 0:
            bb_re, bb_im, l_re, l_im = ssm_prep(ssm_a_re[j], ssm_a_im[j], ssm_log_dt[j], ssm_b_re[j], ssm_b_im[j])
            wb, wc, lre, lim = _ssm_block_weights(bb_re, bb_im, ssm_c_re[j], ssm_c_im[j], l_re, l_im, gb)
            yg = s5_scan(u, wb, lre, lim, wc, ssm_d[j].astype(F32), bsz, steps, cparts)
            h = glu_residual(yg, ssm_w_glu_a[j].astype(BF16), ssm_w_glu_b[j].astype(BF16), h, rows_out, cols_out)
        else:
            gy = conv_in(u, conv_w_in[j].astype(BF16), conv_w[j].astype(F32), bsz, rows, cols)
            h = matmul_residual(gy, conv_w_out[j].astype(BF16), h, rows_out, cols_out)
        last = i == depth - 1
        g_next = norm_final if last else norm_mix[i + 1]
        u_dtype = F32 if (last or (i + 1) % 2 == 0) else BF16
        h, u, row_buf = hier_moe_layer(h, norm_ffn[i], moe_w_group_router[i], moe_w_expert_router[i],
                                       moe_w_gate, moe_w_up, moe_w_down, i, g_next, u_dtype, row_buf,
                                       rows=rows, bm=bm, toks_d=toks_d, toks_c=toks_c,
                                       final_nb=bsz if last else 0)
    return u


def kernel(x, norm_mix, norm_ffn, norm_final, ssm_a_re, ssm_a_im, ssm_log_dt, ssm_b_re, ssm_b_im, ssm_c_re, ssm_c_im, ssm_d, ssm_w_glu_a, ssm_w_glu_b, conv_w_in, conv_w, conv_w_out, moe_w_group_router, moe_w_expert_router, moe_w_gate, moe_w_up, moe_w_down):
    return trunk(x, norm_mix, norm_ffn, norm_final, ssm_a_re, ssm_a_im, ssm_log_dt, ssm_b_re, ssm_b_im,
                 ssm_c_re, ssm_c_im, ssm_d, ssm_w_glu_a, ssm_w_glu_b, conv_w_in, conv_w, conv_w_out,
                 moe_w_group_router, moe_w_expert_router, moe_w_gate, moe_w_up, moe_w_down,
                 rows=512, steps=64, cols=1024, rows_out=1024, cols_out=1024, bm=256, toks_d=1024, toks_c=256,
                 cblock=256, cparts=4)
```

```python
import functools

import jax
import jax.numpy as jnp
from jax import lax
from jax.experimental import pallas as pl
from jax.experimental.pallas import tpu as pltpu

F32 = jnp.float32
BF16 = jnp.bfloat16
I32 = jnp.int32
U32 = jnp.uint32

LANES = 128
SUBLANES = 8
VMEM_LIMIT = 56 * 1024 * 1024
RMS_EPS = 1e-6
TOP_K = 2
DMA_UNROLL = 8


def _cparams(sem):
    return pltpu.CompilerParams(dimension_semantics=sem, vmem_limit_bytes=VMEM_LIMIT)


def _rms(h, g):
    ms = jnp.mean(h * h, axis=-1, keepdims=True)
    return h * lax.rsqrt(ms + RMS_EPS) * g


def _pack_rows(x_bf16_as_f32, k):
    bits = lax.bitcast_convert_type(x_bf16_as_f32[:, (2 * k) * LANES:(2 * k + 2) * LANES], U32)
    return bits[:, :LANES] | (bits[:, LANES:] >> 16)


def _unpack_words(w):
    return (lax.bitcast_convert_type(w & jnp.uint32(0xFFFF0000), F32),
            lax.bitcast_convert_type(w << 16, F32))


def _norm_in_kernel(x_ref, g_ref, h_ref, u_ref, t_ref, *, nb, steps):
    chunks = t_ref.shape[0]
    for b in range(nb):
        xb = x_ref[b]
        for c in range(chunks):
            t_ref[c, pl.ds(b, steps, stride=nb), :] = xb[:, c * LANES:(c + 1) * LANES]
    h = jnp.concatenate([t_ref[c] for c in range(chunks)], axis=-1)
    h_ref[...] = h
    u_ref[...] = _rms(h, g_ref[...])


def norm_in(x, g, steps):
    nb, seq, d = x.shape
    rows = steps * nb
    row_spec = pl.BlockSpec((rows, d), lambda i: (i, 0))
    return pl.pallas_call(
        functools.partial(_norm_in_kernel, nb=nb, steps=steps),
        out_shape=(jax.ShapeDtypeStruct((nb * seq, d), F32), jax.ShapeDtypeStruct((nb * seq, d), F32)),
        grid=(seq // steps,),
        in_specs=[pl.BlockSpec((nb, steps, d), lambda i: (0, i, 0)),
                  pl.BlockSpec((1, d), lambda i: (0, 0))],
        out_specs=(row_spec, row_spec),
        scratch_shapes=[pltpu.VMEM((d // LANES, rows, LANES), F32)],
        compiler_params=_cparams(("parallel",)),
        name="norm_in",
    )(x, g.reshape(1, d))


def _zoh(a_re, a_im, log_dt):
    dt = jnp.exp(log_dt)
    mag = jnp.exp(a_re * dt)
    return mag * jnp.cos(a_im * dt), mag * jnp.sin(a_im * dt)


def _ssm_prep_kernel(are_ref, aim_ref, ldt_ref, bre_ref, bim_ref,
                     are2_ref, aim2_ref, ldt2_ref,
                     bbre_ref, bbim_ref, lre_ref, lim_ref):
    a_re = are_ref[...]
    a_im = aim_ref[...]
    lb_re, lb_im = _zoh(a_re, a_im, ldt_ref[...])
    den = a_re * a_re + a_im * a_im
    nr = lb_re - 1.0
    ni = lb_im
    coef_re = (nr * a_re + ni * a_im) / den
    coef_im = (ni * a_re - nr * a_im) / den
    br = bre_ref[...]
    bi = bim_ref[...]
    bbre_ref[...] = coef_re * br - coef_im * bi
    bbim_ref[...] = coef_re * bi + coef_im * br
    l_re, l_im = _zoh(are2_ref[...], aim2_ref[...], ldt2_ref[...])
    lre_ref[...] = l_re
    lim_ref[...] = l_im


def ssm_prep(a_re, a_im, log_dt, b_re, b_im):
    g, p, hh = b_re.shape
    rep = lambda a: jnp.repeat(a.astype(F32), hh, axis=1)
    ldt2 = jnp.broadcast_to(log_dt.astype(F32)[:, None], (g, p))
    outs = pl.pallas_call(
        _ssm_prep_kernel,
        out_shape=(jax.ShapeDtypeStruct((g, p * hh), F32), jax.ShapeDtypeStruct((g, p * hh), F32),
                   jax.ShapeDtypeStruct((g, p), F32), jax.ShapeDtypeStruct((g, p), F32)),
        name="ssm_prep",
    )(rep(a_re), rep(a_im), rep(ldt2), b_re.astype(F32).reshape(g, p * hh), b_im.astype(F32).reshape(g, p * hh),
      a_re.astype(F32), a_im.astype(F32), ldt2)
    bb_re, bb_im, l_re, l_im = outs
    return bb_re.reshape(g, p, hh), bb_im.reshape(g, p, hh), l_re, l_im


def _ssm_block_weights(bb_re, bb_im, c_re, c_im, l_re, l_im, gb):
    g, p, hh = bb_re.shape
    j = g // gb
    ch_group = jnp.arange(gb * hh, dtype=I32) // hh
    st_group = jnp.arange(gb * p, dtype=I32) // p

    def emb_b(bb):
        t = jnp.transpose(bb.reshape(j, gb, p, hh), (0, 1, 3, 2)).reshape(j, gb * hh, p)
        return jnp.where(ch_group[:, None] == st_group[None, :], jnp.tile(t, (1, 1, gb)), 0.0)

    def emb_c(c):
        t = jnp.transpose(c.reshape(j, gb, hh, p), (0, 3, 1, 2)).reshape(j, p, gb * hh)
        return jnp.where(st_group[:, None] == ch_group[None, :], jnp.tile(t, (1, gb, 1)), 0.0)

    wb = jnp.concatenate([emb_b(bb_re), emb_b(bb_im)], axis=2).astype(BF16)
    wc = jnp.concatenate([emb_c(c_re.astype(F32)), -emb_c(c_im.astype(F32))], axis=1).astype(BF16)
    lre = l_re.reshape(j, 1, gb * p)
    lim = l_im.reshape(j, 1, gb * p)
    return wb, wc, lre, lim


def _s5_kernel(u_ref, wb_ref, lre_ref, lim_ref, wc_ref, d_ref, o_ref,
               bu_ref, sre_ref, sim_ref, *, steps, nb, ns, cb, parts):
    @pl.when(pl.program_id(1) == 0)
    def _():
        sre_ref[...] = jnp.zeros_like(sre_ref)
        sim_ref[...] = jnp.zeros_like(sim_ref)

    def project_in(p):
        u = u_ref[:, p * cb:(p + 1) * cb]
        bu_ref[p] = jnp.dot(u.astype(BF16), wb_ref[p], preferred_element_type=F32)

    def recur(p):
        lr = jnp.broadcast_to(lre_ref[p], (nb, ns))
        li = jnp.broadcast_to(lim_ref[p], (nb, ns))
        sr = sre_ref[p]
        si = sim_ref[p]
        for t in range(steps):
            br = bu_ref[p, t * nb:(t + 1) * nb, 0:ns]
            bi = bu_ref[p, t * nb:(t + 1) * nb, ns:2 * ns]
            sr, si = lr * sr - li * si + br, lr * si + li * sr + bi
            bu_ref[p, t * nb:(t + 1) * nb, 0:ns] = sr
            bu_ref[p, t * nb:(t + 1) * nb, ns:2 * ns] = si
        sre_ref[p] = sr
        sim_ref[p] = si

    def project_out(p):
        y = jnp.dot(bu_ref[p].astype(BF16), wc_ref[p], preferred_element_type=F32)
        y = y + d_ref[:, p * cb:(p + 1) * cb] * u_ref[:, p * cb:(p + 1) * cb]
        o_ref[:, p * cb:(p + 1) * cb] = jax.nn.gelu(y).astype(o_ref.dtype)

    project_in(0)
    for p in range(parts):
        if p + 1 < parts:
            project_in(p + 1)
        if p >= 1:
            project_out(p - 1)
        recur(p)
    project_out(parts - 1)


def s5_scan(u, wb, lre, lim, wc, d_skip, nb, steps, parts):
    n, d = u.shape
    j, cb, ns2 = wb.shape
    ns = ns2 // 2
    rows = steps * nb
    jp = j // parts
    wb = wb.reshape(jp, parts, cb, ns2)
    wc = wc.reshape(jp, parts, ns2, cb)
    lre = lre.reshape(jp, parts, 1, ns)
    lim = lim.reshape(jp, parts, 1, ns)
    return pl.pallas_call(
        functools.partial(_s5_kernel, steps=steps, nb=nb, ns=ns, cb=cb, parts=parts),
        out_shape=jax.ShapeDtypeStruct((n, d), BF16),
        grid=(jp, n // rows),
        in_specs=[pl.BlockSpec((rows, parts * cb), lambda jj, i: (i, jj)),
                  pl.BlockSpec((None, parts, cb, ns2), lambda jj, i: (jj, 0, 0, 0)),
                  pl.BlockSpec((None, parts, 1, ns), lambda jj, i: (jj, 0, 0, 0)),
                  pl.BlockSpec((None, parts, 1, ns), lambda jj, i: (jj, 0, 0, 0)),
                  pl.BlockSpec((None, parts, ns2, cb), lambda jj, i: (jj, 0, 0, 0)),
                  pl.BlockSpec((1, parts * cb), lambda jj, i: (0, jj))],
        out_specs=pl.BlockSpec((rows, parts * cb), lambda jj, i: (i, jj)),
        scratch_shapes=[pltpu.VMEM((parts, rows, ns2), F32),
                        pltpu.VMEM((parts, nb, ns), F32),
                        pltpu.VMEM((parts, nb, ns), F32)],
        compiler_params=_cparams(("parallel", "arbitrary")),
        name="s5_scan",
    )(u, wb, lre, lim, wc, d_skip.reshape(1, d))


def _glu_kernel(y_ref, wa_ref, wb_ref, h_ref, o_ref):
    y = y_ref[...]
    a = jnp.dot(y, wa_ref[...], preferred_element_type=F32)
    b = jnp.dot(y, wb_ref[...], preferred_element_type=F32)
    o_ref[...] = h_ref[...] + a * jax.nn.sigmoid(b)


def glu_residual(y, wa, wb, h, rows, cols):
    n, d = y.shape
    dn = wa.shape[1]
    return pl.pallas_call(
        _glu_kernel,
        out_shape=jax.ShapeDtypeStruct((n, dn), F32),
        grid=(dn // cols, n // rows),
        in_specs=[pl.BlockSpec((rows, d), lambda c, i: (i, 0)),
                  pl.BlockSpec((d, cols), lambda c, i: (0, c)),
                  pl.BlockSpec((d, cols), lambda c, i: (0, c)),
                  pl.BlockSpec((rows, cols), lambda c, i: (i, c))],
        out_specs=pl.BlockSpec((rows, cols), lambda c, i: (i, c)),
        compiler_params=_cparams(("parallel", "parallel")),
        name="glu_residual",
    )(y, wa, wb, h)


def _mm_res_kernel(x_ref, w_ref, h_ref, o_ref):
    o_ref[...] = h_ref[...] + jnp.dot(x_ref[...], w_ref[...], preferred_element_type=F32)


def matmul_residual(x, w, h, rows, cols):
    n, d = x.shape
    dn = w.shape[1]
    return pl.pallas_call(
        _mm_res_kernel,
        out_shape=jax.ShapeDtypeStruct((n, dn), F32),
        grid=(dn // cols, n // rows),
        in_specs=[pl.BlockSpec((rows, d), lambda c, i: (i, 0)),
                  pl.BlockSpec((d, cols), lambda c, i: (0, c)),
                  pl.BlockSpec((rows, cols), lambda c, i: (i, c))],
        out_specs=pl.BlockSpec((rows, cols), lambda c, i: (i, c)),
        compiler_params=_cparams(("parallel", "parallel")),
        name="matmul_residual",
    )(x, w, h)


def _conv_in_kernel(u_ref, wbg_ref, wcg_ref, wv_ref, cw_ref, o_ref, z_ref, *, rows, nb, width):
    halo = (width - 1) * nb

    @pl.when(pl.program_id(1) == 0)
    def _():
        z_ref[0:halo, :] = jnp.zeros((halo, z_ref.shape[1]), F32)

    u = u_ref[...]
    bg = jnp.dot(u, wbg_ref[...], preferred_element_type=F32)
    cg = jnp.dot(u, wcg_ref[...], preferred_element_type=F32)
    v = jnp.dot(u, wv_ref[...], preferred_element_type=F32)
    z_ref[halo:halo + rows, :] = cg * v
    y = cw_ref[0:1, :] * z_ref[0:rows, :]
    for k in range(1, width):
        y = y + cw_ref[k:k + 1, :] * z_ref[k * nb:k * nb + rows, :]
    o_ref[...] = (bg * y).astype(o_ref.dtype)
    z_ref[0:halo, :] = z_ref[rows:rows + halo, :]


def conv_in(u, w_in, conv_w, nb, rows, cols):
    n, d = u.shape
    width = conv_w.shape[0]
    nc = d // cols
    halo = (width - 1) * nb
    return pl.pallas_call(
        functools.partial(_conv_in_kernel, rows=rows, nb=nb, width=width),
        out_shape=jax.ShapeDtypeStruct((n, d), BF16),
        grid=(nc, n // rows),
        in_specs=[pl.BlockSpec((rows, d), lambda c, i: (i, 0)),
                  pl.BlockSpec((d, cols), lambda c, i: (0, c)),
                  pl.BlockSpec((d, cols), lambda c, i: (0, c + nc)),
                  pl.BlockSpec((d, cols), lambda c, i: (0, c + 2 * nc)),
                  pl.BlockSpec((width, cols), lambda c, i: (0, c))],
        out_specs=pl.BlockSpec((rows, cols), lambda c, i: (i, c)),
        scratch_shapes=[pltpu.VMEM((rows + halo, cols), F32)],
        compiler_params=_cparams(("parallel", "arbitrary")),
        name="conv_in",
    )(u, w_in, w_in, w_in, conv_w)


def _router_kernel(h_ref, g_ref, w_ref, tri_ref, xn_ref, mf_ref, mi_ref, cnt_ref,
                   *, rows, d, n_groups, epg):
    @pl.when(pl.program_id(0) == 0)
    def _():
        cnt_ref[...] = jnp.zeros_like(cnt_ref)

    xn = _rms(h_ref[...], g_ref[...])
    x_hi = xn.astype(BF16)
    x_hi32 = x_hi.astype(F32)

    words = d // LANES // 2
    for k in range(words):
        xn_ref[pl.ds(k, rows, stride=words), :] = _pack_rows(x_hi32, k)

    x_lo = (xn - x_hi32).astype(BF16)
    both = jnp.dot(x_hi, w_ref[...], preferred_element_type=F32)
    logits = (both[:, :LANES] + both[:, LANES:]
              + jnp.dot(x_lo, w_ref[:, 0:LANES], preferred_element_type=F32))

    lane = lax.broadcasted_iota(I32, (rows, LANES), 1).astype(F32)
    neg = jnp.float32(-jnp.inf)
    big = jnp.float32(LANES)

    def first_max(vals, mask):
        m = jnp.max(jnp.where(mask, vals, neg), axis=-1, keepdims=True)
        idx = jnp.min(jnp.where(mask & (vals == m), lane, big), axis=-1, keepdims=True)
        return m, idx

    gmask = lane < n_groups
    gmax, gidx = first_max(logits, gmask)
    gsum = jnp.sum(jnp.where(gmask, jnp.exp(logits - gmax), 0.0), axis=-1, keepdims=True)
    g_w = 1.0 / gsum
    lo = n_groups + gidx * epg
    emask = (lane >= lo) & (lane < lo + epg)
    t1, i1 = first_max(logits, emask)
    t2, i2 = first_max(logits, emask & (lane != i1))
    e2 = jnp.exp(t2 - t1)
    p1 = 1.0 / (1.0 + e2)
    p2 = e2 / (1.0 + e2)
    eid1 = i1 - n_groups
    eid2 = i2 - n_groups

    onehot = ((lane == eid1) | (lane == eid2))
    oh = jnp.where(onehot, 1.0, 0.0).astype(BF16)
    before = jnp.dot(tri_ref[...], oh, preferred_element_type=F32) + cnt_ref[...]
    r1 = jnp.sum(jnp.where(lane == eid1, before, 0.0), axis=-1, keepdims=True)
    r2 = jnp.sum(jnp.where(lane == eid2, before, 0.0), axis=-1, keepdims=True)
    cnt_ref[...] = cnt_ref[...] + jnp.sum(oh.astype(F32), axis=0, keepdims=True)

    mf_ref[...] = jnp.where(lane == 0, g_w * p1, jnp.where(lane == 1, g_w * p2, 0.0))
    mi_ref[...] = jnp.where(lane == 0, eid1, jnp.where(lane == 1, eid2,
                            jnp.where(lane == 2, r1, jnp.where(lane == 3, r2, 0.0)))).astype(I32)


def router(h, g, w_group, w_expert, rows):
    n, d = h.shape
    n_groups = w_group.shape[1]
    epg = w_expert.shape[2]
    n_exp = n_groups * epg
    assert n_groups + n_exp <= LANES
    words = d // LANES // 2
    w = jnp.concatenate([w_group.astype(F32),
                         jnp.transpose(w_expert.astype(F32), (1, 0, 2)).reshape(d, n_exp)], axis=1)
    w = jnp.pad(w, ((0, 0), (0, LANES - n_groups - n_exp)))
    w_hi = w.astype(BF16)
    w_lo = (w - w_hi.astype(F32)).astype(BF16)
    w_cat = jnp.concatenate([w_hi, w_lo], axis=1)
    tri = (lax.broadcasted_iota(I32, (rows, rows), 1) < lax.broadcasted_iota(I32, (rows, rows), 0)).astype(BF16)
    xn, mf, mi, cnt = pl.pallas_call(
        functools.partial(_router_kernel, rows=rows, d=d, n_groups=n_groups, epg=epg),
        out_shape=(jax.ShapeDtypeStruct((n * words, LANES), U32),
                   jax.ShapeDtypeStruct((n, LANES), F32),
                   jax.ShapeDtypeStruct((n, LANES), I32),
                   jax.ShapeDtypeStruct((1, LANES), F32)),
        grid=(n // rows,),
        in_specs=[pl.BlockSpec((rows, d), lambda i: (i, 0)),
                  pl.BlockSpec((1, d), lambda i: (0, 0)),
                  pl.BlockSpec((d, 2 * LANES), lambda i: (0, 0)),
                  pl.BlockSpec((rows, rows), lambda i: (0, 0))],
        out_specs=(pl.BlockSpec((rows * words, LANES), lambda i: (i, 0)),
                   pl.BlockSpec((rows, LANES), lambda i: (i, 0)),
                   pl.BlockSpec((rows, LANES), lambda i: (i, 0)),
                   pl.BlockSpec((1, LANES), lambda i: (0, 0))),
        compiler_params=_cparams(("arbitrary",)),
        name="moe_router",
    )(h, g.reshape(1, d), w_cat, tri)
    return xn, mf, mi, cnt


def _dispatch_kernel(dest_ref, xn_ref, xs_in_ref, xs_ref, sem, *, toks, words):
    del xs_in_ref

    def issue(r, c):
        src = xn_ref.at[pl.ds(pl.multiple_of(r * words, words), words)]
        for k in range(TOP_K):
            dst = dest_ref[0, 0, TOP_K * r + k]
            pltpu.make_async_copy(src, xs_ref.at[pl.ds(pl.multiple_of(dst * words, words), words)],
                                  sem).start(priority=k % 2)
        return c

    lax.fori_loop(0, toks, issue, 0, unroll=DMA_UNROLL)

    for k in range(TOP_K):
        pltpu.make_async_copy(xn_ref, xs_ref.at[pl.ds(0, toks * words)], sem).wait()


def moe_dispatch(dest, xn_lin, xs_init, toks, words):
    n = xn_lin.shape[0] // words
    return pl.pallas_call(
        functools.partial(_dispatch_kernel, toks=toks, words=words),
        out_shape=jax.ShapeDtypeStruct(xs_init.shape, xs_init.dtype),
        grid=(n // toks,),
        in_specs=[pl.BlockSpec((1, 1, TOP_K * toks), lambda i: (i, 0, 0), memory_space=pltpu.SMEM),
                  pl.BlockSpec((toks * words, LANES), lambda i: (i, 0)),
                  pl.BlockSpec(memory_space=pl.ANY)],
        out_specs=pl.BlockSpec(memory_space=pl.ANY),
        scratch_shapes=[pltpu.SemaphoreType.DMA(())],
        input_output_aliases={2: 0},
        compiler_params=_cparams(("arbitrary",)),
        name="moe_dispatch",
    )(dest, xn_lin, xs_init)


def _moe_ffn_kernel(be_ref, nu_ref, first_ref, nxt_ref, par_ref, xs_ref, wg_hbm, wu_hbm, wd_hbm, ys_ref,
                    wgs_ref, wus_ref, wds_ref, wgb_ref, wub_ref, wdb_ref, sem, *, layer, bm, words):
    i = pl.program_id(0)

    def weight_copies(e, s):
        return (pltpu.make_async_copy(wg_hbm.at[layer, e], wgs_ref.at[s], sem.at[s, 0]),
                pltpu.make_async_copy(wu_hbm.at[layer, e], wus_ref.at[s], sem.at[s, 1]),
                pltpu.make_async_copy(wd_hbm.at[layer, e], wds_ref.at[s], sem.at[s, 2]))

    @pl.when(i == 0)
    def _():
        for cp in weight_copies(be_ref[0], 0):
            cp.start()

    for s in range(2):
        @pl.when((first_ref[i] == 1) & (par_ref[i] == s))
        def _():
            for cp in weight_copies(be_ref[i], s):
                cp.wait()

            @pl.when(nxt_ref[i] >= 0)
            def _():
                for cp in weight_copies(nxt_ref[i], 1 - s):
                    cp.start()

            wgb_ref[...] = wgs_ref[s].astype(BF16)
            wub_ref[...] = wus_ref[s].astype(BF16)
            wdb_ref[...] = wds_ref[s].astype(BF16)

    @pl.when(i < nu_ref[0])
    def _():
        halves = []
        for k in range(words):
            halves.extend(_unpack_words(xs_ref[pl.ds(k, bm, stride=words), :]))
        x = jnp.concatenate(halves, axis=-1).astype(BF16)
        gt = jnp.dot(x, wgb_ref[...], preferred_element_type=F32)
        up = jnp.dot(x, wub_ref[...], preferred_element_type=F32)
        mid = (jax.nn.silu(gt) * up).astype(BF16)
        y = jnp.dot(mid, wdb_ref[...], preferred_element_type=F32)
        y = y.astype(BF16).astype(F32)
        for k in range(words):
            ys_ref[pl.ds(k, bm, stride=words), :] = _pack_rows(y, k)

    @pl.when(i >= nu_ref[0])
    def _():
        ys_ref[...] = jnp.zeros_like(ys_ref)


def moe_ffn(blk_expert, n_used, xs_lin, w_gate, w_up, w_down, layer, bm, words):
    n_blocks = xs_lin.shape[0] // (bm * words)
    _, _, d, de = w_gate.shape

    def row_map(i, be, nu, *_):
        return (jnp.minimum(i, nu[0] - 1), 0)

    ar = jnp.arange(n_blocks, dtype=I32)
    first = jnp.concatenate([jnp.ones((1,), I32), (blk_expert[1:] != blk_expert[:-1]).astype(I32)])
    parity = (jnp.cumsum(first) - 1) % 2
    start_pos = jnp.where(first == 1, ar, n_blocks)
    next_start = jnp.concatenate([lax.cummin(start_pos, reverse=True)[1:], jnp.full((1,), n_blocks, I32)])
    nxt = jnp.where(next_start < n_blocks, blk_expert[jnp.minimum(next_start, n_blocks - 1)], -1).astype(I32)

    any_spec = pl.BlockSpec(memory_space=pl.ANY)
    return pl.pallas_call(
        functools.partial(_moe_ffn_kernel, layer=layer, bm=bm, words=words),
        out_shape=jax.ShapeDtypeStruct(xs_lin.shape, U32),
        grid_spec=pltpu.PrefetchScalarGridSpec(
            num_scalar_prefetch=5,
            grid=(n_blocks,),
            in_specs=[pl.BlockSpec((bm * words, LANES), row_map), any_spec, any_spec, any_spec],
            out_specs=pl.BlockSpec((bm * words, LANES), lambda i, *_: (i, 0)),
            scratch_shapes=[pltpu.VMEM((2, d, de), F32), pltpu.VMEM((2, d, de), F32), pltpu.VMEM((2, de, d), F32),
                            pltpu.VMEM((d, de), BF16), pltpu.VMEM((d, de), BF16), pltpu.VMEM((de, d), BF16),
                            pltpu.SemaphoreType.DMA((2, 3))]),
        compiler_params=_cparams(("arbitrary",)),
        name="moe_ffn",
    )(blk_expert, n_used, first, nxt, parity.astype(I32), xs_lin, w_gate, w_up, w_down)


def _slab_pitch(words):
    return words + 2


def _combine_kernel(dc_ref, dn_ref, ys_ref, h_ref, mf_ref, g_ref, *rest, toks, words, n_tiles, nb):
    if nb:
        fo_ref, buf_ref, ub_ref, sem = rest
    else:
        ho_ref, uo_ref, buf_ref, sem = rest
    i = pl.program_id(0)
    chunks = 2 * words
    slab = _slab_pitch(words)
    slot_rows = toks * TOP_K * slab
    stride = TOP_K * slab

    def start_tile(dest_ref, slot):
        def body(r, c):
            for k in range(TOP_K):
                src = dest_ref[0, 0, TOP_K * r + k]
                pltpu.make_async_copy(ys_ref.at[pl.ds(pl.multiple_of(src * words, words), words)],
                                      buf_ref.at[pl.ds(slot * slot_rows + (r * TOP_K + k) * slab, words)],
                                      sem.at[slot]).start(priority=k % 2)
            return c

        lax.fori_loop(0, toks, body, 0, unroll=DMA_UNROLL)

    def wait_tile(slot):
        rows = toks * TOP_K * words
        pltpu.make_async_copy(ys_ref.at[pl.ds(0, rows)], buf_ref.at[pl.ds(slot * slot_rows, rows)],
                              sem.at[slot]).wait()

    def run(slot):
        if slot == 0:
            @pl.when(i == 0)
            def _():
                start_tile(dc_ref, 0)

        @pl.when(i + 1 < n_tiles)
        def _():
            start_tile(dn_ref, 1 - slot)

        wait_tile(slot)

        def expert_rows(k_slot):
            base = slot * slot_rows + k_slot * slab
            halves = []
            for k in range(words):
                halves.extend(_unpack_words(buf_ref[pl.ds(base + k, toks, stride=stride), :]))
            return jnp.concatenate(halves, axis=-1)

        mf = mf_ref[...]
        h = h_ref[...] + mf[:, 0:1] * expert_rows(0) + mf[:, 1:2] * expert_rows(1)
        if nb:
            u = _rms(h, g_ref[...])
            for c in range(chunks):
                ub_ref[c] = u[:, c * LANES:(c + 1) * LANES]
            for b in range(nb):
                fo_ref[b] = jnp.concatenate(
                    [ub_ref[c, pl.ds(b, toks // nb, stride=nb), :] for c in range(chunks)], axis=-1)
        else:
            ho_ref[...] = h
            uo_ref[...] = _rms(h, g_ref[...]).astype(uo_ref.dtype)

    @pl.when(i % 2 == 0)
    def _():
        run(0)

    @pl.when(i % 2 == 1)
    def _():
        run(1)


def moe_combine(dest, ys_lin, h, mf, g_next, u_dtype, toks, words, nb=0):
    n, d = h.shape
    n_tiles = n // toks
    row_spec = pl.BlockSpec((toks, d), lambda i: (i, 0))
    gather_buf = pltpu.VMEM((2 * toks * TOP_K * _slab_pitch(words), LANES), U32)
    if nb:
        out_shape = jax.ShapeDtypeStruct((nb, n // nb, d), F32)
        out_specs = pl.BlockSpec((nb, toks // nb, d), lambda i: (0, i, 0))
        scratch = [gather_buf, pltpu.VMEM((2 * words, toks, LANES), F32), pltpu.SemaphoreType.DMA((2,))]
    else:
        out_shape = (jax.ShapeDtypeStruct((n, d), F32), jax.ShapeDtypeStruct((n, d), u_dtype))
        out_specs = (row_spec, row_spec)
        scratch = [gather_buf, pltpu.SemaphoreType.DMA((2,))]
    return pl.pallas_call(
        functools.partial(_combine_kernel, toks=toks, words=words, n_tiles=n_tiles, nb=nb),
        out_shape=out_shape,
        grid=(n_tiles,),
        in_specs=[pl.BlockSpec((1, 1, TOP_K * toks), lambda i: (i, 0, 0), memory_space=pltpu.SMEM),
                  pl.BlockSpec((1, 1, TOP_K * toks), lambda i: (jnp.minimum(i + 1, n_tiles - 1), 0, 0),
                               memory_space=pltpu.SMEM),
                  pl.BlockSpec(memory_space=pl.ANY),
                  row_spec,
                  pl.BlockSpec((toks, LANES), lambda i: (i, 0)),
                  pl.BlockSpec((1, d), lambda i: (0, 0))],
        out_specs=out_specs,
        scratch_shapes=scratch,
        compiler_params=_cparams(("arbitrary",)),
        name="moe_combine",
    )(dest, dest, ys_lin, h, mf, g_next.reshape(1, d))


def hier_moe_layer(h, g_ffn, w_group, w_expert, w_gate, w_up, w_down, layer, g_next, u_dtype, row_buf,
                   *, rows, bm, toks_d, toks_c, final_nb=0):
    n, d = h.shape
    words = d // LANES // 2
    n_exp = w_gate.shape[1]
    xn_lin, mf, mi, cnt = router(h, g_ffn, w_group, w_expert, rows)

    counts = cnt[0, :n_exp].astype(I32)
    padded = (counts + bm - 1) // bm * bm
    pad_end = jnp.cumsum(padded)
    pad_start = pad_end - padded
    n_rows = n * TOP_K + n_exp * bm
    n_blocks = n_rows // bm
    blk_start = jnp.arange(n_blocks, dtype=I32) * bm
    blk_expert = jnp.sum((blk_start[:, None] >= pad_end[None, :]).astype(I32), axis=1)
    blk_expert = jnp.minimum(blk_expert, n_exp - 1).astype(I32)
    n_used = (pad_end[-1:] // bm).astype(I32)
    last_e = blk_expert[jnp.maximum(n_used[0] - 1, 0)]
    blk_expert = jnp.where(jnp.arange(n_blocks) < n_used[0], blk_expert, last_e)

    eid = mi[:, 0:TOP_K]
    seg = jnp.sum(jnp.where(eid[:, :, None] == jnp.arange(n_exp, dtype=I32), pad_start, 0), axis=-1)
    dest = (seg + mi[:, TOP_K:2 * TOP_K]).astype(I32)
    if row_buf is None:
        row_buf = jnp.zeros((n_rows * words, LANES), U32)
    xs_lin = moe_dispatch(dest.reshape(n // toks_d, 1, TOP_K * toks_d), xn_lin, row_buf, toks_d, words)
    ys_lin = moe_ffn(blk_expert, n_used, xs_lin, w_gate, w_up, w_down, layer, bm, words)
    dest_c = dest.reshape(n // toks_c, 1, TOP_K * toks_c)
    if final_nb:
        return None, moe_combine(dest_c, ys_lin, h, mf, g_next, u_dtype, toks_c, words, final_nb), None
    h_out, u_out = moe_combine(dest_c, ys_lin, h, mf, g_next, u_dtype, toks_c, words)
    return h_out, u_out, xs_lin


def trunk(x, norm_mix, norm_ffn, norm_final, ssm_a_re, ssm_a_im, ssm_log_dt, ssm_b_re, ssm_b_im,
          ssm_c_re, ssm_c_im, ssm_d, ssm_w_glu_a, ssm_w_glu_b, conv_w_in, conv_w, conv_w_out,
          moe_w_group_router, moe_w_expert_router, moe_w_gate, moe_w_up, moe_w_down,
          *, rows, steps, cols, rows_out, cols_out, bm, toks_d, toks_c, cblock, cparts):
    bsz, seq, d = x.shape
    depth = norm_mix.shape[0]
    hh = ssm_b_re.shape[3]
    gb = cblock // hh

    h, u = norm_in(x, norm_mix[0], steps)
    row_buf = None
    for i in range(depth):
        j = i // 2
        if i % 2 == 0:
            bb_re, bb_im, l_re, l_im = ssm_prep(ssm_a_re[j], ssm_a_im[j], ssm_log_dt[j], ssm_b_re[j], ssm_b_im[j])
            wb, wc, lre, lim = _ssm_block_weights(bb_re, bb_im, ssm_c_re[j], ssm_c_im[j], l_re, l_im, gb)
            yg = s5_scan(u, wb, lre, lim, wc, ssm_d[j].astype(F32), bsz, steps, cparts)
            h = glu_residual(yg, ssm_w_glu_a[j].astype(BF16), ssm_w_glu_b[j].astype(BF16), h, rows_out, cols_out)
        else:
            gy = conv_in(u, conv_w_in[j].astype(BF16), conv_w[j].astype(F32), bsz, rows, cols)
            h = matmul_residual(gy, conv_w_out[j].astype(BF16), h, rows_out, cols_out)
        last = i == depth - 1
        g_next = norm_final if last else norm_mix[i + 1]
        u_dtype = F32 if (last or (i + 1) % 2 == 0) else BF16
        h, u, row_buf = hier_moe_layer(h, norm_ffn[i], moe_w_group_router[i], moe_w_expert_router[i],
                                       moe_w_gate, moe_w_up, moe_w_down, i, g_next, u_dtype, row_buf,
                                       rows=rows, bm=bm, toks_d=toks_d, toks_c=toks_c,
                                       final_nb=bsz if last else 0)
    return u


def kernel(x, norm_mix, norm_ffn, norm_final, ssm_a_re, ssm_a_im, ssm_log_dt, ssm_b_re, ssm_b_im, ssm_c_re, ssm_c_im, ssm_d, ssm_w_glu_a, ssm_w_glu_b, conv_w_in, conv_w, conv_w_out, moe_w_group_router, moe_w_expert_router, moe_w_gate, moe_w_up, moe_w_down):
    return trunk(x, norm_mix, norm_ffn, norm_final, ssm_a_re, ssm_a_im, ssm_log_dt, ssm_b_re, ssm_b_im,
                 ssm_c_re, ssm_c_im, ssm_d, ssm_w_glu_a, ssm_w_glu_b, conv_w_in, conv_w, conv_w_out,
                 moe_w_group_router, moe_w_expert_router, moe_w_gate, moe_w_up, moe_w_down,
                 rows=512, steps=64, cols=1024, rows_out=1024, cols_out=1024, bm=256, toks_d=1024, toks_c=256,
                 cblock=256, cparts=4)
```
